```python
import math
import jax, jax.numpy as jnp
from jax import lax
import numpy as np

D_MODEL = 1024
BATCH = 8
SEQ = 2048
DEPTH = 2
DEC_BATCH = 32
DEC_SEQ = 4
PAST_LEN = 16384
PAGE_SIZE = 128

HEAD_DIM = 64
N_HEADS = D_MODEL // HEAD_DIM
SB_HEADS = N_HEADS // 2
NSA_HEADS = N_HEADS - SB_HEADS
NSA_KV = 2
NSA_REP = NSA_HEADS // NSA_KV
D_MIX = N_HEADS * HEAD_DIM
SB_W = SB_HEADS * HEAD_DIM
NSA_W = NSA_HEADS * HEAD_DIM
KV_W = NSA_KV * HEAD_DIM
GATE_W = 3 * NSA_HEADS
IN_SPLITS = (SB_W, SB_W, SB_W, NSA_W, KV_W, KV_W, KV_W, KV_W, KV_W, KV_W, GATE_W)
IN_OFFSETS = tuple(int(v) for v in np.cumsum(IN_SPLITS)[:-1])
D_IN = sum(IN_SPLITS)
CMP_STRIDE = 16
CMP_LEN = 2 * CMP_STRIDE
SLC_BLOCK = 64
SLC_TOP = 16
WINDOW = 512
N_BUCKETS = 32
MAX_DISTANCE = 1024
D_FF = 4 * D_MODEL
Q_BLOCK = 128
RMS_EPS = 1e-6
NEG_INF = -1e30
FORCE = 1e9
TINY = 1e-30

kernel_name = 'hymba_stickbreak_nsa_decoder_step'


def rms_norm(x, g):
    xf = x.astype(jnp.float32)
    y = xf * lax.rsqrt(jnp.mean(xf * xf, axis=-1, keepdims=True) + RMS_EPS)
    return (y * g.astype(jnp.float32)).astype(x.dtype)


def masked_softmax(logits, mask):
    lg = jnp.where(mask, logits, NEG_INF)
    m = jnp.max(lg, axis=-1, keepdims=True)
    e = jnp.where(mask, jnp.exp(lg - m), 0.0)
    return e / jnp.maximum(jnp.sum(e, axis=-1, keepdims=True), TINY)


def t5_bucket(dist):
    dist = jnp.maximum(dist, 0)
    exact = N_BUCKETS // 2
    ratio = jnp.log(jnp.maximum(dist, 1).astype(jnp.float32) / exact) / math.log(MAX_DISTANCE / exact)
    large = jnp.minimum(exact + (ratio * (N_BUCKETS - exact)).astype(jnp.int32), N_BUCKETS - 1)
    return jnp.where(dist < exact, dist, large)


def rel_bias_2d(table, dist):
    return table[t5_bucket(dist)].transpose(2, 3, 0, 1)


def compress(x_all, pe, w1, w2):
    b, l = x_all.shape[:2]
    n_chunk = l // CMP_STRIDE
    chunks = x_all[:, :n_chunk * CMP_STRIDE].reshape(b, n_chunk, CMP_STRIDE, NSA_KV, HEAD_DIM)
    w1r = w1.reshape(CMP_LEN, HEAD_DIM, HEAD_DIM)
    u_lo = jnp.einsum('bcpgd,pde->bcge', chunks, w1r[:CMP_STRIDE])
    u_hi = jnp.einsum('bcpgd,pde->bcge', chunks, w1r[CMP_STRIDE:])
    pe_term = jnp.einsum('pd,pde->e', pe, w1r)
    h = jax.nn.gelu(u_lo[:, :-1] + u_hi[:, 1:] + pe_term)
    return h @ w2


def to_blocks(a, n_blk):
    b, l = a.shape[:2]
    a = jnp.pad(a, ((0, 0), (0, n_blk * SLC_BLOCK - l), (0, 0), (0, 0)))
    return a.reshape(b, n_blk, SLC_BLOCK, NSA_KV, HEAD_DIM).transpose(0, 3, 1, 2, 4)


def last_rows(a, n):
    l = a.shape[1]
    if l < n:
        a = jnp.pad(a, ((0, 0), (n - l, 0)) + ((0, 0),) * (a.ndim - 2))
    return a[:, a.shape[1] - n:]


def token_mixers(q_sb, q_nsa, gates, sb_all, nsa_all, win_all, offset, w_pos0, cmp_pe, cmp_w1, cmp_w2, rel_bias):
    b, t = q_sb.shape[:2]
    l = sb_all.shape[1]
    scale = HEAD_DIM ** -0.5
    k_sb, v_sb = sb_all[:, :, 0], sb_all[:, :, 1]
    kc = compress(nsa_all[:, :, 0], cmp_pe[0], cmp_w1[0], cmp_w2[0])
    vc = compress(nsa_all[:, :, 1], cmp_pe[1], cmp_w1[1], cmp_w2[1])
    n_cmp = kc.shape[1]
    c_start = jnp.arange(n_cmp) * CMP_STRIDE
    c_end = c_start + (CMP_LEN - 1)
    n_slc = -(-l // SLC_BLOCK)
    s_start = jnp.arange(n_slc) * SLC_BLOCK
    overlap = ((c_end[:, None] >= s_start[None, :]) &
               (c_start[:, None] <= s_start[None, :] + SLC_BLOCK - 1)).astype(jnp.float32)
    ks_blk = to_blocks(nsa_all[:, :, 2], n_slc)
    vs_blk = to_blocks(nsa_all[:, :, 3], n_slc)
    pad_w = ((0, 0), (WINDOW, 0), (0, 0), (0, 0))
    kw_pad = jnp.pad(win_all[:, :, 0], pad_w)
    vw_pad = jnp.pad(win_all[:, :, 1], pad_w)
    table = rel_bias.astype(jnp.float32).reshape(N_BUCKETS, NSA_KV, NSA_REP)
    n_top = min(SLC_TOP, n_slc)
    n_sel = n_top * SLC_BLOCK
    qb = Q_BLOCK if t % Q_BLOCK == 0 else t
    nb = t // qb
    key_pos = jnp.arange(l)
    blk = jnp.arange(n_slc)
    b_ix = jnp.arange(b)[:, None, None, None]
    g_ix = jnp.arange(NSA_KV)[None, :, None, None]

    def block(i):
        start = i * qb
        t_pos = offset + start + jnp.arange(qb)
        qs = lax.dynamic_slice_in_dim(q_sb, start, qb, axis=1)
        z = jnp.einsum('bqhd,blhd->bhql', qs, k_sb, preferred_element_type=jnp.float32) * scale
        causal = key_pos[None, :] < t_pos[:, None]
        log_beta = jax.nn.log_sigmoid(z)
        log_keep = jnp.where(causal, log_beta - z, 0.0)
        log_keep = lax.cumsum(log_keep, axis=3, reverse=True) - log_keep
        att = jnp.where(causal, jnp.exp(log_beta + log_keep), 0.0)
        o_sb = jnp.einsum('bhql,blhd->bqhd', att.astype(v_sb.dtype), v_sb)
        qn = lax.dynamic_slice_in_dim(q_nsa, start, qb, axis=1)
        gate = jax.nn.sigmoid(lax.dynamic_slice_in_dim(gates, start, qb, axis=1).astype(jnp.float32))
        gate = gate.reshape(b, qb, NSA_KV, NSA_REP, 3)
        lc = jnp.einsum('bqgrd,bcgd->bgrqc', qn, kc, preferred_element_type=jnp.float32) * scale
        lc = lc + rel_bias_2d(table, t_pos[:, None] - c_end[None, :])
        pc = masked_softmax(lc, c_end[None, :] <= t_pos[:, None])
        oc = jnp.einsum('bgrqc,bcgd->bqgrd', pc.astype(vc.dtype), vc)
        imp = jnp.einsum('bgrqc,cj->bgqj', pc, overlap)
        cur = t_pos // SLC_BLOCK
        valid = blk[None, :] <= cur[:, None]
        forced = valid & ((blk[None, :] == 0) | (blk[None, :] >= cur[:, None] - 1))
        score = jnp.where(forced, FORCE, jnp.where(valid, imp, -FORCE))
        idx = lax.top_k(score, n_top)[1]
        kg = ks_blk[b_ix, g_ix, idx]
        vg = vs_blk[b_ix, g_ix, idx]
        kpos = idx[..., None] * SLC_BLOCK + jnp.arange(SLC_BLOCK)
        tq = t_pos[None, None, :, None, None]
        ms = (idx <= cur[None, None, :, None])[..., None] & (kpos <= tq)
        bias_s = jnp.moveaxis(table[t5_bucket(tq - kpos), g_ix[..., None]], -1, 2)
        ls = jnp.einsum('bqgrd,bgqnsd->bgrqns', qn, kg, preferred_element_type=jnp.float32) * scale + bias_s
        ps = masked_softmax(ls.reshape(b, NSA_KV, NSA_REP, qb, n_sel), ms.reshape(b, NSA_KV, 1, qb, n_sel))
        o_s = jnp.einsum('bgrqm,bgqmd->bqgrd', ps.astype(vg.dtype), vg.reshape(b, NSA_KV, qb, n_sel, HEAD_DIM))
        w0 = offset + start - w_pos0
        kw = lax.dynamic_slice_in_dim(kw_pad, w0, WINDOW + qb, axis=1)
        vw = lax.dynamic_slice_in_dim(vw_pad, w0, WINDOW + qb, axis=1)
        kwpos = offset + start - WINDOW + jnp.arange(WINDOW + qb)
        dist = t_pos[:, None] - kwpos[None, :]
        mw = (kwpos[None, :] >= w_pos0) & (dist >= 0) & (dist < WINDOW)
        lw = jnp.einsum('bqgrd,bkgd->bgrqk', qn, kw, preferred_element_type=jnp.float32) * scale
        pw = masked_softmax(lw + rel_bias_2d(table, dist), mw)
        o_w = jnp.einsum('bgrqk,bkgd->bqgrd', pw.astype(vw.dtype), vw)
        o_nsa = gate[..., 0:1] * oc + gate[..., 1:2] * o_s + gate[..., 2:3] * o_w
        return (o_sb.astype(q_sb.dtype), o_nsa.reshape(b, qb, NSA_HEADS, HEAD_DIM).astype(q_sb.dtype))

    o_sb, o_nsa = lax.map(block, jnp.arange(nb))
    o = jnp.concatenate([o_sb, o_nsa], axis=3)
    return jnp.moveaxis(o, 0, 1).reshape(b, t, D_MIX)


def mixer_sublayer(h, past_sb, past_nsa, past_win, offset, lp):
    b, t, _ = h.shape
    parts = jnp.split(h @ lp['w_in'], IN_OFFSETS, axis=-1)
    q_sb, k_sb, v_sb = [p.reshape(b, t, SB_HEADS, HEAD_DIM) for p in parts[:3]]
    q_nsa = parts[3].reshape(b, t, NSA_KV, NSA_REP, HEAD_DIM)
    kc, vc, ks, vs, kw, vw = [p.reshape(b, t, NSA_KV, HEAD_DIM) for p in parts[4:10]]
    gates = parts[10].reshape(b, t, NSA_HEADS, 3)
    sb_rows = jnp.stack([k_sb, v_sb], axis=2)
    nsa_rows = jnp.stack([kc, vc, ks, vs], axis=2)
    win_rows = jnp.stack([kw, vw], axis=2)
    if past_sb is None:
        sb_all, nsa_all, win_all = sb_rows, nsa_rows, win_rows
        w_buf = min(WINDOW, PAST_LEN)
        w_pos0 = offset
    else:
        sb_all = jnp.concatenate([past_sb, sb_rows], axis=1)
        nsa_all = jnp.concatenate([past_nsa, nsa_rows], axis=1)
        win_all = jnp.concatenate([past_win, win_rows], axis=1)
        w_buf = past_win.shape[1]
        w_pos0 = offset - w_buf
    o = token_mixers(q_sb, q_nsa, gates, sb_all, nsa_all, win_all, offset, w_pos0,
                     lp['cmp_pe'], lp['cmp_w1'], lp['cmp_w2'], lp['rel_bias'])
    return o @ lp['w_out'], sb_rows, nsa_rows, last_rows(win_all, w_buf)


def decoder_layer(x, c, past_sb, past_nsa, past_win, offset, lp):
    mod = jax.nn.silu(c) @ lp['w_ada'] + lp['b_ada']
    sh1, sc1, gt1, sh2, sc2, gt2 = jnp.split(mod[:, None, :], 6, axis=-1)
    h = rms_norm(x, lp['g_pre_mix']) * (1.0 + sc1) + sh1
    mix, sb_rows, nsa_rows, win_state = mixer_sublayer(h, past_sb, past_nsa, past_win, offset, lp)
    x = x + gt1 * rms_norm(mix, lp['g_post_mix'])
    h = rms_norm(x, lp['g_pre_ffn']) * (1.0 + sc2) + sh2
    f = jnp.square(jax.nn.relu(h @ lp['w_up'])) @ lp['w_down']
    x = x + gt2 * rms_norm(f, lp['g_post_ffn'])
    return x, sb_rows, nsa_rows, win_state


def setup_inputs(seed: int = 0) -> dict:
    key = jax.random.key(seed)
    ks = jax.random.split(key, 24)
    f32 = jnp.float32
    n_pages = PAST_LEN // PAGE_SIZE
    n_used = DEC_BATCH * n_pages
    n_phys = n_used + max(1, n_used // 4)
    w_buf = min(WINDOW, PAST_LEN)

    def nrm(k, shape, s):
        return s * jax.random.normal(k, shape, f32)

    perm = jax.random.permutation(ks[7], n_phys).astype(jnp.int32)
    return {
        'x_prompt': jax.random.normal(ks[0], (BATCH, SEQ, D_MODEL), f32),
        'x_sample': jax.random.normal(ks[1], (DEC_BATCH, DEC_SEQ, D_MODEL), f32),
        'c_prompt': jax.random.normal(ks[2], (BATCH, D_MODEL), f32),
        'c_sample': jax.random.normal(ks[3], (DEC_BATCH, D_MODEL), f32),
        'cache_sb_kv': jax.random.normal(ks[4], (DEPTH, n_phys, PAGE_SIZE, 2, SB_HEADS, HEAD_DIM), f32),
        'cache_nsa_kv': jax.random.normal(ks[5], (DEPTH, n_phys, PAGE_SIZE, 4, NSA_KV, HEAD_DIM), f32),
        'state_nsa_win': jax.random.normal(ks[6], (DEPTH, DEC_BATCH, w_buf, 2, NSA_KV, HEAD_DIM), f32),
        'page_table': perm[:n_used].reshape(DEC_BATCH, n_pages),
        'w_ada': nrm(ks[8], (DEPTH, D_MODEL, 6 * D_MODEL), 0.5 * D_MODEL ** -0.5),
        'b_ada': nrm(ks[9], (DEPTH, 6 * D_MODEL), 0.01),
        'g_pre_mix': 1.0 + nrm(ks[10], (DEPTH, D_MODEL), 0.05),
        'g_post_mix': 1.0 + nrm(ks[11], (DEPTH, D_MODEL), 0.05),
        'g_pre_ffn': 1.0 + nrm(ks[12], (DEPTH, D_MODEL), 0.05),
        'g_post_ffn': 1.0 + nrm(ks[13], (DEPTH, D_MODEL), 0.05),
        'w_in': nrm(ks[14], (DEPTH, D_MODEL, D_IN), D_MODEL ** -0.5),
        'w_out': nrm(ks[15], (DEPTH, D_MIX, D_MODEL), D_MIX ** -0.5),
        'cmp_pe': nrm(ks[16], (DEPTH, 2, CMP_LEN, HEAD_DIM), 0.5),
        'cmp_w1': nrm(ks[17], (DEPTH, 2, CMP_LEN * HEAD_DIM, HEAD_DIM), (CMP_LEN * HEAD_DIM) ** -0.5),
        'cmp_w2': nrm(ks[18], (DEPTH, 2, HEAD_DIM, HEAD_DIM), HEAD_DIM ** -0.5),
        'rel_bias': nrm(ks[19], (N_BUCKETS, NSA_HEADS), 0.5),
        'w_up': nrm(ks[20], (DEPTH, D_MODEL, D_FF), D_MODEL ** -0.5),
        'w_down': nrm(ks[21], (DEPTH, D_FF, D_MODEL), D_FF ** -0.5),
    }


def reference(x_prompt, x_sample, c_prompt, c_sample, cache_sb_kv, cache_nsa_kv, state_nsa_win, page_table,
              w_ada, b_ada, g_pre_mix, g_post_mix, g_pre_ffn, g_post_ffn, w_in, w_out,
              cmp_pe, cmp_w1, cmp_w2, rel_bias, w_up, w_down):
    n_pages = PAST_LEN // PAGE_SIZE
    n_dec = x_sample.shape[0]
    y_p, y_s = x_prompt, x_sample
    sb_p, sb_s, nsa_p, nsa_s, win_p, win_s = [], [], [], [], [], []
    for l in range(DEPTH):
        lp = {'w_ada': w_ada[l], 'b_ada': b_ada[l], 'g_pre_mix': g_pre_mix[l], 'g_post_mix': g_post_mix[l],
              'g_pre_ffn': g_pre_ffn[l], 'g_post_ffn': g_post_ffn[l], 'w_in': w_in[l], 'w_out': w_out[l],
              'cmp_pe': cmp_pe[l], 'cmp_w1': cmp_w1[l], 'cmp_w2': cmp_w2[l], 'rel_bias': rel_bias,
              'w_up': w_up[l], 'w_down': w_down[l]}
        y_p, a, bb, cc = decoder_layer(y_p, c_prompt, None, None, None, 0, lp)
        sb_p.append(a)
        nsa_p.append(bb)
        win_p.append(cc)
        past_sb = cache_sb_kv[l, page_table].reshape(n_dec, n_pages * PAGE_SIZE, 2, SB_HEADS, HEAD_DIM)
        past_nsa = cache_nsa_kv[l, page_table].reshape(n_dec, n_pages * PAGE_SIZE, 4, NSA_KV, HEAD_DIM)
        y_s, a, bb, cc = decoder_layer(y_s, c_sample, past_sb, past_nsa, state_nsa_win[l], PAST_LEN, lp)
        sb_s.append(a)
        nsa_s.append(bb)
        win_s.append(cc)
    return (y_p, y_s, jnp.stack(sb_p), jnp.stack(sb_s), jnp.stack(nsa_p), jnp.stack(nsa_s),
            jnp.stack(win_p), jnp.stack(win_s))
```

```python
import functools
import math

import jax
import jax.numpy as jnp
import numpy as np
from jax import lax
from jax.experimental import pallas as pl
from jax.experimental.pallas import tpu as pltpu

F32 = jnp.float32
BF16 = jnp.bfloat16
SDS = jax.ShapeDtypeStruct

HEAD_DIM = 64
SB_HEADS = 8
NSA_KV = 2
NSA_REP = 4
NSA_HEADS = NSA_KV * NSA_REP
SB_W = SB_HEADS * HEAD_DIM
NSA_W = NSA_HEADS * HEAD_DIM
KV_W = NSA_KV * HEAD_DIM
GATE_W = 3 * NSA_HEADS
D_IN = 3 * SB_W + NSA_W + 6 * KV_W + GATE_W
CMP_STRIDE = 16
CMP_LEN = 2 * CMP_STRIDE
SLC_BLOCK = 64
SLC_TOP = 16
WINDOW = 512
N_BUCKETS = 32
MAX_DISTANCE = 1024
RMS_EPS = 1e-6
NEG_INF = -1e30
FORCE = 1e9
TINY = 1e-30
SCALE = HEAD_DIM ** -0.5

LANES = 128
VMEM_LIMIT = 52 * 1024 * 1024


def _cparams(sem):
    return pltpu.CompilerParams(dimension_semantics=sem, vmem_limit_bytes=VMEM_LIMIT)


def _dot(a, b):
    return jnp.dot(a.astype(BF16), b.astype(BF16), preferred_element_type=F32)


def _dot_nt(a, b):
    return lax.dot_general(a.astype(BF16), b.astype(BF16), (((1,), (1,)), ((), ())),
                           preferred_element_type=F32)


def _split(a):
    hi = a.astype(BF16)
    lo = (a - hi.astype(F32)).astype(BF16)
    return hi, lo


def _dot_hl(a, b_bf16):
    hi, lo = _split(a)
    return (jnp.dot(hi, b_bf16, preferred_element_type=F32)
            + jnp.dot(lo, b_bf16, preferred_element_type=F32))


def _dot3(a, b):
    a_hi, a_lo = _split(a)
    b_hi, b_lo = _split(b)
    return (jnp.dot(a_hi, b_hi, preferred_element_type=F32)
            + jnp.dot(a_hi, b_lo, preferred_element_type=F32)
            + jnp.dot(a_lo, b_hi, preferred_element_type=F32))


def _sigmoid(x):
    return 1.0 / (1.0 + jnp.exp(-x))


def _rms(x, g):
    return x * lax.rsqrt(jnp.mean(x * x, axis=-1, keepdims=True) + RMS_EPS) * g


def _masked_softmax(lg, mask):
    lg = jnp.where(mask, lg, NEG_INF)
    m = jnp.max(lg, axis=-1, keepdims=True)
    e = jnp.where(mask, jnp.exp(lg - m), 0.0)
    return e / jnp.maximum(jnp.sum(e, axis=-1, keepdims=True), TINY)


def _stick_terms(z):
    sp = jnp.log1p(jnp.exp(-jnp.abs(z)))
    lb = jnp.minimum(z, 0.0) - sp
    return lb, lb - z


def _gelu_tanh(x):
    return 0.5 * x * (1.0 + jnp.tanh(math.sqrt(2.0 / math.pi) * (x + 0.044715 * (x * x * x))))


def _ada_kernel(c_ref, w_ref, b_ref, o_ref):
    c = c_ref[...]
    o_ref[0] = _dot3(c * _sigmoid(c), w_ref[0]) + b_ref[0]


def _ada(c_all, w_ada, b_ada):
    depth, d, d6 = w_ada.shape
    nb = c_all.shape[0]
    tn = d6 // 4
    return pl.pallas_call(
        _ada_kernel, out_shape=SDS((depth, nb, d6), F32), grid=(depth, d6 // tn),
        in_specs=[pl.BlockSpec((nb, d), lambda l, j: (0, 0)),
                  pl.BlockSpec((1, d, tn), lambda l, j: (l, 0, j)),
                  pl.BlockSpec((1, 1, tn), lambda l, j: (l, 0, j))],
        out_specs=pl.BlockSpec((1, nb, tn), lambda l, j: (l, 0, j)),
        compiler_params=_cparams(("arbitrary", "arbitrary")), name="ada",
    )(c_all, w_ada, b_ada.reshape(depth, 1, d6))


def _mod_spec(mod3, d, tm, chunk):
    nb, rb, _ = mod3.shape
    if rb == 1:
        rows_per_b = mod3.rows_per_b
        return pl.BlockSpec((1, 1, d), lambda i, *_: ((i * tm) // rows_per_b, 0, chunk))
    return pl.BlockSpec((1, tm, d), lambda i, *_: (0, i, chunk))


class _Mod:
    def __init__(self, arr, rows_per_b):
        self.arr = arr
        self.shape = arr.shape
        self.rows_per_b = rows_per_b


_PROJ_SPLITS = (("q_sb", 0, SB_W), ("sb_rows", SB_W, 3 * SB_W), ("q_nsa", 3 * SB_W, 3 * SB_W + NSA_W),
                ("nsa_rows", 3 * SB_W + NSA_W, 3 * SB_W + NSA_W + 4 * KV_W),
                ("win_rows", 3 * SB_W + NSA_W + 4 * KV_W, 3 * SB_W + NSA_W + 6 * KV_W),
                ("gates", 3 * SB_W + NSA_W + 6 * KV_W, 3 * SB_W + NSA_W + 6 * KV_W + LANES))
D_IN_PAD = _PROJ_SPLITS[-1][2]


def _in_proj_kernel(x_ref, g_ref, sc_ref, sh_ref, w_ref, *out_refs):
    h = _rms(x_ref[...], g_ref[...]) * (1.0 + sc_ref[0]) + sh_ref[0]
    p = jnp.dot(h.astype(BF16), w_ref[...], preferred_element_type=F32)
    for o_ref, (_, lo, hi) in zip(out_refs, _PROJ_SPLITS):
        o_ref[...] = p[:, lo:hi]


def _in_proj(x2d, g, mod, w_bf, tm):
    r, d = x2d.shape
    widths = [hi - lo for _, lo, hi in _PROJ_SPLITS]
    return pl.pallas_call(
        _in_proj_kernel, out_shape=[SDS((r, w), F32) for w in widths], grid=(r // tm,),
        in_specs=[pl.BlockSpec((tm, d), lambda i: (i, 0)),
                  pl.BlockSpec((1, d), lambda i: (0, 0)),
                  _mod_spec(mod, d, tm, 1), _mod_spec(mod, d, tm, 0),
                  pl.BlockSpec((d, D_IN_PAD), lambda i: (0, 0))],
        out_specs=[pl.BlockSpec((tm, w), lambda i: (i, 0)) for w in widths],
        compiler_params=_cparams(("arbitrary",)), name="in_proj",
    )(x2d, g.reshape(1, d), mod.arr, mod.arr, w_bf)


def _out_proj_kernel(osb_ref, onsa_ref, x_ref, g_ref, gt_ref, w_ref, o_ref):
    mix = (jnp.dot(osb_ref[...].astype(BF16), w_ref[0:SB_W, :], preferred_element_type=F32)
           + jnp.dot(onsa_ref[...].astype(BF16), w_ref[SB_W:SB_W + NSA_W, :], preferred_element_type=F32))
    o_ref[...] = x_ref[...] + gt_ref[0] * _rms(mix, g_ref[...])


def _out_proj(o_sb, o_nsa, x2d, g, mod, w_bf, tm):
    r, d = x2d.shape
    return pl.pallas_call(
        _out_proj_kernel, out_shape=SDS((r, d), F32), grid=(r // tm,),
        in_specs=[pl.BlockSpec((tm, SB_W), lambda i: (i, 0)),
                  pl.BlockSpec((tm, NSA_W), lambda i: (i, 0)),
                  pl.BlockSpec((tm, d), lambda i: (i, 0)),
                  pl.BlockSpec((1, d), lambda i: (0, 0)),
                  _mod_spec(mod, d, tm, 2),
                  pl.BlockSpec((SB_W + NSA_W, d), lambda i: (0, 0))],
        out_specs=pl.BlockSpec((tm, d), lambda i: (i, 0)),
        compiler_params=_cparams(("arbitrary",)), name="out_proj",
    )(o_sb, o_nsa, x2d, g.reshape(1, d), mod.arr, w_bf)


def _ffn_kernel(x_ref, gpre_ref, sc_ref, sh_ref, wup_ref, wdn_ref, gpost_ref, gt_ref, o_ref, h_scr, acc_scr):
    k = pl.program_id(1)

    @pl.when(k == 0)
    def _():
        h = _rms(x_ref[...], gpre_ref[...]) * (1.0 + sc_ref[0]) + sh_ref[0]
        h_scr[...] = h.astype(BF16)
        acc_scr[...] = jnp.zeros_like(acc_scr)

    u = jnp.dot(h_scr[...], wup_ref[...], preferred_element_type=F32)
    a = jnp.square(jnp.maximum(u, 0.0))
    acc_scr[...] += jnp.dot(a.astype(BF16), wdn_ref[...], preferred_element_type=F32)

    @pl.when(k == pl.num_programs(1) - 1)
    def _():
        o_ref[...] = x_ref[...] + gt_ref[0] * _rms(acc_scr[...], gpost_ref[...])


def _ffn(x2d, g_pre, g_post, mod, wup_bf, wdn_bf, tm, tf):
    r, d = x2d.shape
    f = wup_bf.shape[1]
    return pl.pallas_call(
        _ffn_kernel, out_shape=SDS((r, d), F32), grid=(r // tm, f // tf),
        in_specs=[pl.BlockSpec((tm, d), lambda i, k: (i, 0)),
                  pl.BlockSpec((1, d), lambda i, k: (0, 0)),
                  _mod_spec(mod, d, tm, 4), _mod_spec(mod, d, tm, 3),
                  pl.BlockSpec((d, tf), lambda i, k: (0, k)),
                  pl.BlockSpec((tf, d), lambda i, k: (k, 0)),
                  pl.BlockSpec((1, d), lambda i, k: (0, 0)),
                  _mod_spec(mod, d, tm, 5)],
        out_specs=pl.BlockSpec((tm, d), lambda i, k: (i, 0)),
        scratch_shapes=[pltpu.VMEM((tm, d), BF16), pltpu.VMEM((tm, d), F32)],
        compiler_params=_cparams(("arbitrary", "arbitrary")), name="ffn",
    )(x2d, g_pre.reshape(1, d), mod.arr, mod.arr, wup_bf, wdn_bf, g_post.reshape(1, d), mod.arr)


def _tri(n):
    j = np.arange(n)
    return jnp.asarray((j[:, None] > j[None, :]).astype(np.float32), dtype=BF16)


def _sb_prompt_kernel(q_ref, k_ref, v_ref, tri_ref, o_ref, *, tq):
    qi = pl.program_id(2)
    q2 = q_ref[0] * SCALE
    lane = lax.broadcasted_iota(jnp.int32, (tq, LANES), 1)
    row = lax.broadcasted_iota(jnp.int32, (tq, tq), 0)
    col = lax.broadcasted_iota(jnp.int32, (tq, tq), 1)
    causal = col < row
    tri = tri_ref[...]
    outs = []
    for h in range(2):
        qm = jnp.where((lane >= HEAD_DIM * h) & (lane < HEAD_DIM * (h + 1)), q2, 0.0).astype(BF16)

        def tile(c, carry, acc, diagonal):
            k_c = k_ref[0, pl.ds(pl.multiple_of(c * tq, tq), tq), :]
            v_c = v_ref[0, pl.ds(pl.multiple_of(c * tq, tq), tq), :]
            lb, lk = _stick_terms(_dot_nt(qm, k_c))
            if diagonal:
                lk = jnp.where(causal, lk, 0.0)
            tot = _dot_hl(lk, tri) + carry
            att = jnp.exp(lb + tot)
            if diagonal:
                att = jnp.where(causal, att, 0.0)
            acc = acc + _dot(att, v_c)
            carry = carry + jnp.sum(lk, axis=1, keepdims=True)
            return carry, acc

        carry, acc = tile(qi, jnp.zeros((tq, 1), F32), jnp.zeros((tq, LANES), F32), True)
        carry, acc = lax.fori_loop(
            0, qi, lambda i, ca: tile(qi - 1 - i, ca[0], ca[1], False), (carry, acc))
        outs.append(acc)
    o_ref[0] = jnp.where(lane < HEAD_DIM, outs[0], outs[1])


def _sb_prompt(q_sb, sb_rows, tq):
    b, t, _ = q_sb.shape
    npair = SB_W // LANES
    return pl.pallas_call(
        functools.partial(_sb_prompt_kernel, tq=tq), out_shape=SDS((b, t, SB_W), F32),
        grid=(b, npair, t // tq),
        in_specs=[pl.BlockSpec((1, tq, LANES), lambda bi, hp, qi: (bi, qi, hp)),
                  pl.BlockSpec((1, t, LANES), lambda bi, hp, qi: (bi, 0, hp)),
                  pl.BlockSpec((1, t, LANES), lambda bi, hp, qi: (bi, 0, npair + hp)),
                  pl.BlockSpec((tq, tq), lambda bi, hp, qi: (0, 0))],
        out_specs=pl.BlockSpec((1, tq, LANES), lambda bi, hp, qi: (bi, qi, hp)),
        compiler_params=_cparams(("arbitrary", "arbitrary", "arbitrary")), name="sb_prompt",
    )(q_sb, sb_rows, sb_rows, _tri(tq))


def _compress_weights(cmp_pe, cmp_w1, cmp_w2):
    w1r = cmp_w1.reshape(2, CMP_LEN, HEAD_DIM, HEAD_DIM)
    lo, hi = w1r[:, :CMP_STRIDE], w1r[:, CMP_STRIDE:]
    z = jnp.zeros_like(lo)
    top = jnp.concatenate([lo, z, hi, z], axis=-1)
    bot = jnp.concatenate([z, lo, z, hi], axis=-1)
    w1p = jnp.concatenate([top, bot], axis=2).astype(BF16)
    z2 = jnp.zeros_like(cmp_w2)
    w2p = jnp.concatenate([jnp.concatenate([cmp_w2, z2], axis=-1),
                           jnp.concatenate([z2, cmp_w2], axis=-1)], axis=1).astype(BF16)
    pe = jnp.broadcast_to(cmp_pe.reshape(2, 1, CMP_LEN * HEAD_DIM), (2, 8, CMP_LEN * HEAD_DIM))
    w1cat = jnp.concatenate([cmp_w1, cmp_w1], axis=-1)
    return w1p, w2p, pe, w1cat


def _compress_partial(slab_ref, w1p_ref, kind, n):
    u = jnp.zeros((n, 2 * LANES), F32)
    for p in range(CMP_STRIDE):
        x = slab_ref[pl.ds(p, n, stride=CMP_STRIDE), :]
        u = u + jnp.dot(x.astype(BF16), w1p_ref[kind, p], preferred_element_type=F32)
    return u


def _compress_finish(u, kind, w2p_ref, pe_ref, w1cat_ref):
    n = u.shape[0]
    pe_term = _dot3(pe_ref[kind], w1cat_ref[kind])[0:1]
    nxt = pltpu.roll(u[:, LANES:], n - 1, 0)
    h = _gelu_tanh(u[:, :LANES] + nxt + pe_term)
    return jnp.dot(h.astype(BF16), w2p_ref[kind], preferred_element_type=F32)


def _cmp_prompt_kernel(k_ref, v_ref, w1p_ref, w2p_ref, pe_ref, w1cat_ref, o_ref, *, n):
    for kind, ref in enumerate((k_ref, v_ref)):
        u = _compress_partial(ref.at[0], w1p_ref, kind, n)
        o_ref[0, :, kind * LANES:(kind + 1) * LANES] = _compress_finish(u, kind, w2p_ref, pe_ref, w1cat_ref)


def _cmp_prompt(nsa_rows, cw):
    b, t, _ = nsa_rows.shape
    n = t // CMP_STRIDE
    w1p, w2p, pe, w1cat = cw
    full = lambda a: pl.BlockSpec(a.shape, lambda bi: (0,) * a.ndim)
    return pl.pallas_call(
        functools.partial(_cmp_prompt_kernel, n=n), out_shape=SDS((b, n, 2 * LANES), F32), grid=(b,),
        in_specs=[pl.BlockSpec((1, t, LANES), lambda bi: (bi, 0, 0)),
                  pl.BlockSpec((1, t, LANES), lambda bi: (bi, 0, 1)),
                  full(w1p), full(w2p), full(pe), full(w1cat)],
        out_specs=pl.BlockSpec((1, n, 2 * LANES), lambda bi: (bi, 0, 0)),
        compiler_params=_cparams(("arbitrary",)), name="cmp_prompt",
    )(nsa_rows, nsa_rows, w1p, w2p, pe, w1cat)


def _t5_bucket(dist):
    dist = jnp.maximum(dist, 0)
    exact = N_BUCKETS // 2
    ratio = jnp.log(jnp.maximum(dist, 1).astype(F32) / exact) / math.log(MAX_DISTANCE / exact)
    large = jnp.minimum(exact + (ratio * (N_BUCKETS - exact)).astype(jnp.int32), N_BUCKETS - 1)
    return jnp.where(dist < exact, dist, large)


def _bias_at(rel_bias, dist):
    return jnp.moveaxis(rel_bias.astype(F32)[_t5_bucket(dist)], -1, 0)


def _overlap(n_c, n_slc, n_slc_pad):
    c = np.arange(n_c)
    j = np.arange(n_slc_pad)
    c_start, c_end = c * CMP_STRIDE, c * CMP_STRIDE + CMP_LEN - 1
    s_start = j * SLC_BLOCK
    ov = ((c_end[:, None] >= s_start[None, :]) & (c_start[:, None] <= s_start[None, :] + SLC_BLOCK - 1)
          & (c[:, None] < n_c - 1) & (j[None, :] < n_slc))
    return jnp.asarray(ov.astype(np.float32), dtype=BF16)


def _expand(n_blk_pad, n_keys, first_block=0):
    blk = np.arange(n_blk_pad)
    s = np.arange(n_keys)
    return jnp.asarray((s[None, :] // SLC_BLOCK == blk[:, None] + first_block).astype(np.float32), dtype=BF16)


def _gate_expand():
    e = np.zeros((NSA_KV, LANES, 3 * NSA_REP * HEAD_DIM), np.float32)
    for g in range(NSA_KV):
        for r in range(NSA_REP):
            for j in range(3):
                e[g, (g * NSA_REP + r) * 3 + j, j * NSA_REP * HEAD_DIM + r * HEAD_DIM:][:HEAD_DIM] = 1.0
    return jnp.asarray(e, dtype=BF16)


def _top_blocks(imp, cur, n_slc, n_top):
    tq = imp.shape[0]
    blk = lax.broadcasted_iota(jnp.int32, (tq, LANES), 1)
    valid = blk <= cur
    forced = valid & ((blk == 0) | (blk >= cur - 1))
    score = jnp.where(forced, FORCE, jnp.where(valid, imp, -FORCE))
    rank = jnp.zeros((tq, LANES), F32)
    for i in range(n_slc):
        si = score[:, i:i + 1]
        beats = (si > score) | ((si == score) & (blk > i))
        rank = rank + jnp.where(beats, 1.0, 0.0)
    return jnp.where((rank < n_top) & valid, 1.0, 0.0)


def _nsa_prompt_kernel(q_ref, kc_ref, vc_ref, ks_ref, vs_ref, kw_ref, vw_ref, gate_ref, ge_ref, bc_ref,
                       strip_ref, ovl_ref, ex_ref, o_ref, *, tq, tk, n_c, n_slc, n_top, m_off):
    qi = pl.program_id(2)
    q0 = qi * tq
    r4 = NSA_REP * tq
    q = (q_ref[0, 0] * SCALE).reshape(r4, HEAD_DIM).astype(BF16)
    t_pos = q0 + lax.broadcasted_iota(jnp.int32, (tq, 1), 0)

    c_idx = lax.broadcasted_iota(jnp.int32, (1, n_c), 1)
    mc = ((CMP_STRIDE * c_idx + CMP_LEN - 1) <= t_pos) & (c_idx < n_c - 1)
    lc = _dot_nt(q, kc_ref[0, 0]).reshape(NSA_REP, tq, n_c) + bc_ref[0]
    pc = _masked_softmax(lc, mc[None])
    oc = _dot(pc.reshape(r4, n_c), vc_ref[0, 0])

    imp = _dot_hl(jnp.sum(pc, axis=0), ovl_ref[...])
    sel = _top_blocks(imp, t_pos >> 6, n_slc, n_top).astype(BF16)

    def slc_tile(c, carry):
        m, l, acc = carry
        s0 = pl.multiple_of(c * tk, tk)
        ls = _dot_nt(q, ks_ref[0, 0, pl.ds(s0, tk), :]).reshape(NSA_REP, tq, tk)
        ls = ls + strip_ref[0, :, :, pl.ds(pl.multiple_of(s0 - q0 + m_off, LANES), tk)]
        selk = jnp.dot(sel, ex_ref[:, pl.ds(s0, tk)], preferred_element_type=F32)
        s_pos = s0 + lax.broadcasted_iota(jnp.int32, (1, tk), 1)
        mk = ((selk > 0.5) & (s_pos <= t_pos))[None]
        lg = jnp.where(mk, ls, NEG_INF)
        m_new = jnp.maximum(m, jnp.max(lg, axis=-1, keepdims=True))
        alpha = jnp.exp(m - m_new)
        p = jnp.where(mk, jnp.exp(lg - m_new), 0.0)
        l = alpha * l + jnp.sum(p, axis=-1, keepdims=True)
        pv = _dot(p.reshape(r4, tk), vs_ref[0, 0, pl.ds(s0, tk), :]).reshape(NSA_REP, tq, HEAD_DIM)
        return m_new, l, alpha * acc + pv

    n_tiles = (q0 + tq + tk - 1) // tk
    m, l, acc = lax.fori_loop(0, n_tiles, slc_tile,
                              (jnp.full((NSA_REP, tq, 1), NEG_INF, F32), jnp.zeros((NSA_REP, tq, 1), F32),
                               jnp.zeros((NSA_REP, tq, HEAD_DIM), F32)))
    o_s = (acc / jnp.maximum(l, TINY)).reshape(r4, HEAD_DIM)

    band = WINDOW + tq
    w0 = pl.multiple_of(jnp.maximum(q0 - WINDOW, 0), LANES)
    lw = _dot_nt(q, kw_ref[0, 0, pl.ds(w0, band), :]).reshape(NSA_REP, tq, band)
    lw = lw + strip_ref[0, :, :, pl.ds(pl.multiple_of(w0 - q0 + m_off, LANES), band)]
    dist = t_pos - (w0 + lax.broadcasted_iota(jnp.int32, (1, band), 1))
    pw = _masked_softmax(lw, ((dist >= 0) & (dist < WINDOW))[None])
    o_w = _dot(pw.reshape(r4, band), vw_ref[0, 0, pl.ds(w0, band), :])

    gexp = _dot_hl(_sigmoid(gate_ref[0]), ge_ref[0])
    gw = NSA_REP * HEAD_DIM

    def token_major(o):
        return jnp.concatenate([o[r * tq:(r + 1) * tq] for r in range(NSA_REP)], axis=1)

    o_ref[0] = (gexp[:, 0:gw] * token_major(oc) + gexp[:, gw:2 * gw] * token_major(o_s)
                + gexp[:, 2 * gw:3 * gw] * token_major(o_w))


def _nsa_prompt(q_nsa, nsa_rows, win_rows, gates, kcvc, rel_bias, tq, tk):
    b, t, _ = q_nsa.shape
    n_c = t // CMP_STRIDE
    n_slc = -(-t // SLC_BLOCK)
    n_top = min(SLC_TOP, n_slc)
    assert n_slc <= LANES and t % tk == 0 and t >= WINDOW + tq
    g_, r_ = NSA_KV, NSA_REP
    qh = q_nsa.reshape(b, t, g_, r_, HEAD_DIM).transpose(0, 2, 3, 1, 4)
    nsah = nsa_rows.reshape(b, t, 4, g_, HEAD_DIM).transpose(2, 0, 3, 1, 4)
    winh = win_rows.reshape(b, t, 2, g_, HEAD_DIM).transpose(2, 0, 3, 1, 4)
    kch = kcvc.reshape(b, n_c, 2, g_, HEAD_DIM).transpose(2, 0, 3, 1, 4)
    m_off = t - tq
    width = m_off + max(tk, WINDOW + tq)
    tt = jnp.arange(t)
    bias_c = _bias_at(rel_bias, tt[:, None] - (jnp.arange(n_c)[None, :] * CMP_STRIDE + CMP_LEN - 1))
    bias_c = bias_c.reshape(g_, r_, t, n_c)
    strip = _bias_at(rel_bias, jnp.arange(tq)[:, None] + m_off - jnp.arange(width)[None, :])
    strip = strip.reshape(g_, r_, tq, width)
    kv_spec = pl.BlockSpec((1, 1, t, HEAD_DIM), lambda bi, g, qi: (bi, g, 0, 0))
    kc_spec = pl.BlockSpec((1, 1, n_c, HEAD_DIM), lambda bi, g, qi: (bi, g, 0, 0))
    ovl = _overlap(n_c, n_slc, LANES)
    ex = _expand(LANES, t)
    ge = _gate_expand()
    kern = functools.partial(_nsa_prompt_kernel, tq=tq, tk=tk, n_c=n_c, n_slc=n_slc, n_top=n_top, m_off=m_off)
    return pl.pallas_call(
        kern, out_shape=SDS((b, t, NSA_W), F32), grid=(b, g_, t // tq),
        in_specs=[pl.BlockSpec((1, 1, r_, tq, HEAD_DIM), lambda bi, g, qi: (bi, g, 0, qi, 0)),
                  kc_spec, kc_spec, kv_spec, kv_spec, kv_spec, kv_spec,
                  pl.BlockSpec((1, tq, LANES), lambda bi, g, qi: (bi, qi, 0)),
                  pl.BlockSpec((1,) + ge.shape[1:], lambda bi, g, qi: (g, 0, 0)),
                  pl.BlockSpec((1, r_, tq, n_c), lambda bi, g, qi: (g, 0, qi, 0)),
                  pl.BlockSpec((1, r_, tq, width), lambda bi, g, qi: (g, 0, 0, 0)),
                  pl.BlockSpec(ovl.shape, lambda bi, g, qi: (0, 0)),
                  pl.BlockSpec(ex.shape, lambda bi, g, qi: (0, 0))],
        out_specs=pl.BlockSpec((1, tq, r_ * HEAD_DIM), lambda bi, g, qi: (bi, qi, g)),
        compiler_params=_cparams(("arbitrary", "arbitrary", "arbitrary")), name="nsa_prompt",
    )(qh, kch[0], kch[1], nsah[2], nsah[3], winh[0], winh[1], gates, ge, bias_c, strip, ovl, ex)


def _page_copies(cache_ref, layer, pt_ref, seq, first_page, n_pages, slabs, buf, slot, sem, page_rows):
    copies = []
    for i in range(n_pages):
        page = pt_ref[seq, first_page + i]
        for s, (lane0, width) in enumerate(slabs):
            copies.append(pltpu.make_async_copy(
                cache_ref.at[layer, page, :, pl.ds(lane0, width)],
                buf.at[slot, s, pl.ds(i * page_rows, page_rows), :], sem.at[slot]))
    return copies


def _paged_pipeline(cache_ref, layer, pt_ref, n_seq, n_chunk, chunk_pages, slabs, buf, sem, page_rows,
                    reverse):
    seq, j = pl.program_id(0), pl.program_id(1)
    step = seq * n_chunk + j

    def start(st):
        s_, j_ = st // n_chunk, st % n_chunk
        first = ((n_chunk - 1 - j_) if reverse else j_) * chunk_pages
        for cp in _page_copies(cache_ref, layer, pt_ref, s_, first, chunk_pages, slabs, buf, st % 2, sem,
                               page_rows):
            cp.start()

    @pl.when(step == 0)
    def _():
        start(step)

    @pl.when(step + 1 < n_seq * n_chunk)
    def _():
        start(step + 1)

    slot = step % 2
    for cp in _page_copies(cache_ref, layer, pt_ref, 0, 0, chunk_pages, slabs, buf, slot, sem, page_rows):
        cp.wait()
    return slot


def _sb_sample_kernel(pt_ref, qbd_ref, new_ref, tri_ref, cache_ref, o_ref, buf, sem, carry_scr, acc_scr,
                      *, layer, n_seq, n_chunk, chunk_pages, page_rows, ts):
    j = pl.program_id(1)
    slot = _paged_pipeline(cache_ref, layer, pt_ref, n_seq, n_chunk, chunk_pages, ((0, 2 * SB_W),), buf, sem,
                           page_rows, reverse=True)
    rows = ts * SB_HEADS
    qbd = (qbd_ref[0] * SCALE).astype(BF16)
    tri = tri_ref[...]
    sub = tri.shape[0]

    def cumsum_tiles(lb, lk, carry, mask):
        atts = []
        for s in reversed(range(lb.shape[1] // sub)):
            sl = slice(s * sub, (s + 1) * sub)
            att = jnp.exp(lb[:, sl] + _dot_hl(lk[:, sl], tri) + carry)
            atts.append(att if mask is None else jnp.where(mask[:, sl], att, 0.0))
            carry = carry + jnp.sum(lk[:, sl], axis=1, keepdims=True)
        return jnp.concatenate(atts[::-1], axis=1), carry

    @pl.when(j == 0)
    def _():
        n_new = new_ref.shape[1]
        lb, lk = _stick_terms(_dot_nt(qbd, new_ref[0, :, 0:SB_W]))
        q_idx = lax.broadcasted_iota(jnp.int32, (rows, n_new), 0) // SB_HEADS
        causal = lax.broadcasted_iota(jnp.int32, (rows, n_new), 1) < q_idx
        lk = jnp.where(causal, lk, 0.0)
        att, carry = cumsum_tiles(lb, lk, jnp.zeros((rows, 1), F32), causal)
        carry_scr[...] = carry
        acc_scr[...] = _dot(att, new_ref[0, :, SB_W:2 * SB_W])

    kv = buf.at[slot, 0]
    lb, lk = _stick_terms(_dot_nt(qbd, kv[:, 0:SB_W]))
    att, carry = cumsum_tiles(lb, lk, carry_scr[...], None)
    carry_scr[...] = carry
    acc_scr[...] += _dot(att, kv[:, SB_W:2 * SB_W])

    @pl.when(j == n_chunk - 1)
    def _():
        acc = acc_scr[...]
        head_of_row = lax.broadcasted_iota(jnp.int32, acc.shape, 0) % SB_HEADS
        head_of_lane = lax.broadcasted_iota(jnp.int32, acc.shape, 1) // HEAD_DIM
        own = jnp.where(head_of_row == head_of_lane, acc, 0.0)
        o_ref[0] = jnp.sum(own.reshape(ts, SB_HEADS, SB_W), axis=1)


def _sb_sample(layer, q_sb, sb_rows, cache_sb, page_table, chunk_pages, sub):
    bs, ts, _ = q_sb.shape
    n_pages = page_table.shape[1]
    page_rows = cache_sb.shape[2]
    n_chunk = n_pages // chunk_pages
    rows = ts * SB_HEADS
    eye = jnp.eye(SB_HEADS, dtype=F32)
    qbd = jnp.einsum("bqhd,hk->bqhkd", q_sb.reshape(bs, ts, SB_HEADS, HEAD_DIM), eye).reshape(bs, rows, SB_W)
    new = jnp.pad(sb_rows, ((0, 0), (0, sub - ts), (0, 0)))
    kern = functools.partial(_sb_sample_kernel, layer=layer, n_seq=bs, n_chunk=n_chunk,
                             chunk_pages=chunk_pages, page_rows=page_rows, ts=ts)
    grid_spec = pltpu.PrefetchScalarGridSpec(
        num_scalar_prefetch=1, grid=(bs, n_chunk),
        in_specs=[pl.BlockSpec((1, rows, SB_W), lambda b, j, pt: (b, 0, 0)),
                  pl.BlockSpec((1, sub, 2 * SB_W), lambda b, j, pt: (b, 0, 0)),
                  pl.BlockSpec((sub, sub), lambda b, j, pt: (0, 0)),
                  pl.BlockSpec(memory_space=pl.ANY)],
        out_specs=pl.BlockSpec((1, ts, SB_W), lambda b, j, pt: (b, 0, 0)),
        scratch_shapes=[pltpu.VMEM((2, 1, chunk_pages * page_rows, 2 * SB_W), F32),
                        pltpu.SemaphoreType.DMA((2,)),
                        pltpu.VMEM((rows, 1), F32), pltpu.VMEM((rows, SB_W), F32)])
    return pl.pallas_call(
        kern, out_shape=SDS((bs, ts, SB_W), F32), grid_spec=grid_spec,
        compiler_params=_cparams(("arbitrary", "arbitrary")), name="sb_sample",
    )(page_table, qbd, new, _tri(sub), cache_sb)


def _cmp_sample_kernel(pt_ref, w1p_ref, w2p_ref, pe_ref, w1cat_ref, cache_ref, o_ref, buf, sem, u_scr,
                       *, layer, n_seq, n_chunk, chunk_pages, page_rows):
    j = pl.program_id(1)
    slot = _paged_pipeline(cache_ref, layer, pt_ref, n_seq, n_chunk, chunk_pages,
                           ((0, LANES), (LANES, LANES)), buf, sem, page_rows, reverse=False)
    n_step = chunk_pages * page_rows // CMP_STRIDE
    for kind in range(2):
        u_scr[kind, pl.ds(pl.multiple_of(j * n_step, n_step), n_step), :] = _compress_partial(
            buf.at[slot, kind], w1p_ref, kind, n_step)

    @pl.when(j == n_chunk - 1)
    def _():
        for kind in range(2):
            o_ref[0, :, kind * LANES:(kind + 1) * LANES] = _compress_finish(
                u_scr[kind], kind, w2p_ref, pe_ref, w1cat_ref)


def _cmp_sample(layer, cache_nsa, page_table, cw, chunk_pages):
    bs, n_pages = page_table.shape
    page_rows = cache_nsa.shape[2]
    n_chunk = n_pages // chunk_pages
    n_c = n_pages * page_rows // CMP_STRIDE
    w1p, w2p, pe, w1cat = cw
    full = lambda a: pl.BlockSpec(a.shape, lambda b, j, pt: (0,) * a.ndim)
    kern = functools.partial(_cmp_sample_kernel, layer=layer, n_seq=bs, n_chunk=n_chunk,
                             chunk_pages=chunk_pages, page_rows=page_rows)
    grid_spec = pltpu.PrefetchScalarGridSpec(
        num_scalar_prefetch=1, grid=(bs, n_chunk),
        in_specs=[full(w1p), full(w2p), full(pe), full(w1cat), pl.BlockSpec(memory_space=pl.ANY)],
        out_specs=pl.BlockSpec((1, n_c, 2 * LANES), lambda b, j, pt: (b, 0, 0)),
        scratch_shapes=[pltpu.VMEM((2, 2, chunk_pages * page_rows, LANES), F32),
                        pltpu.SemaphoreType.DMA((2,)),
                        pltpu.VMEM((2, n_c, 2 * LANES), F32)])
    return pl.pallas_call(
        kern, out_shape=SDS((bs, n_c, 2 * LANES), F32), grid_spec=grid_spec,
        compiler_params=_cparams(("arbitrary", "arbitrary")), name="cmp_sample",
    )(page_table, w1p, w2p, pe, w1cat, cache_nsa)


def _rank_rows(score, n_top):
    rows, nb = score.shape
    ii = lax.broadcasted_iota(jnp.int32, (nb, nb), 0)
    jj = lax.broadcasted_iota(jnp.int32, (nb, nb), 1)
    out = []
    for r in range(rows):
        s_row = score[r:r + 1, :]
        s_col = jnp.sum(jnp.where(ii == jj, s_row, 0.0), axis=1, keepdims=True)
        beats = (s_col > s_row) | ((s_col == s_row) & (ii < jj))
        rank = jnp.sum(jnp.where(beats, 1.0, 0.0), axis=0, keepdims=True)
        out.append(jnp.where(rank < n_top, 1.0, 0.0))
    return jnp.concatenate(out, axis=0)


def _nsa_a_kernel(qbd_ref, kcvc_ref, win_ref, wnew_ref, bc_ref, bwp_ref, bwn_ref, ovl_ref,
                  oc_ref, ow_ref, sel_ref, *, ts, past, n_slc, n_top):
    rows = NSA_REP * NSA_KV * ts
    gq = NSA_KV * ts
    qbd = (qbd_ref[0] * SCALE).astype(BF16)
    q_idx = lax.broadcasted_iota(jnp.int32, (rows, 1), 0) % ts
    t_pos = past + q_idx

    n_c = kcvc_ref.shape[1]
    c_idx = lax.broadcasted_iota(jnp.int32, (1, n_c), 1)
    mc = ((CMP_STRIDE * c_idx + CMP_LEN - 1) <= t_pos) & (c_idx < n_c - 1)
    pc = _masked_softmax(_dot_nt(qbd, kcvc_ref[0, :, 0:KV_W]) + bc_ref[...], mc)
    oc_ref[0] = _dot(pc, kcvc_ref[0, :, KV_W:2 * KV_W])

    pcs = pc[0:gq]
    for r in range(1, NSA_REP):
        pcs = pcs + pc[r * gq:(r + 1) * gq]
    imp = _dot_hl(pcs, ovl_ref[...])
    nb = imp.shape[1]
    blk = lax.broadcasted_iota(jnp.int32, (gq, nb), 1)
    cur = (past + lax.broadcasted_iota(jnp.int32, (gq, 1), 0) % ts) >> 6
    valid = blk <= cur
    forced = valid & ((blk == 0) | (blk >= cur - 1))
    score = jnp.where(forced, FORCE, jnp.where(valid, imp, -FORCE))
    sel_ref[0] = jnp.where(valid, _rank_rows(score, n_top), 0.0)

    w_buf = win_ref.shape[2]
    r_idx = lax.broadcasted_iota(jnp.int32, (1, w_buf), 1)
    d_past = w_buf + q_idx - r_idx
    l1 = _dot_nt(qbd, win_ref[0, 0, :, 0:KV_W]) + bwp_ref[...]
    m1 = (d_past >= 0) & (d_past < WINDOW)
    n_new = wnew_ref.shape[1]
    d_new = q_idx - lax.broadcasted_iota(jnp.int32, (1, n_new), 1)
    l2 = _dot_nt(qbd, wnew_ref[0, :, 0:KV_W]) + bwn_ref[...]
    m2 = (d_new >= 0) & (d_new < WINDOW)
    l1 = jnp.where(m1, l1, NEG_INF)
    l2 = jnp.where(m2, l2, NEG_INF)
    m = jnp.maximum(jnp.max(l1, axis=1, keepdims=True), jnp.max(l2, axis=1, keepdims=True))
    e1 = jnp.where(m1, jnp.exp(l1 - m), 0.0)
    e2 = jnp.where(m2, jnp.exp(l2 - m), 0.0)
    den = jnp.maximum(jnp.sum(e1, axis=1, keepdims=True) + jnp.sum(e2, axis=1, keepdims=True), TINY)
    ow_ref[0] = _dot(e1 / den, win_ref[0, 0, :, KV_W:2 * KV_W]) + _dot(e2 / den, wnew_ref[0, :, KV_W:2 * KV_W])


def _nsa_b_kernel(pt_ref, qbd_ref, snew_ref, sel_ref, selnew_ref, bs_ref, bsn_ref, ex_ref, oc_ref, ow_ref,
                  gate_ref, cache_ref, o_ref, buf, sem, m_scr, l_scr, acc_scr,
                  *, layer, n_seq, n_chunk, chunk_pages, page_rows, ts):
    j = pl.program_id(1)
    slot = _paged_pipeline(cache_ref, layer, pt_ref, n_seq, n_chunk, chunk_pages,
                           ((2 * KV_W, KV_W), (3 * KV_W, KV_W)), buf, sem, page_rows, reverse=False)
    rows = NSA_REP * NSA_KV * ts
    qbd = (qbd_ref[0] * SCALE).astype(BF16)
    q_idx = lax.broadcasted_iota(jnp.int32, (rows, 1), 0) % ts

    def online(lg, mk, v):
        lg = jnp.where(mk, lg, NEG_INF)
        m_old = m_scr[...]
        m_new = jnp.maximum(m_old, jnp.max(lg, axis=1, keepdims=True))
        alpha = jnp.exp(m_old - m_new)
        p = jnp.where(mk, jnp.exp(lg - m_new), 0.0)
        m_scr[...] = m_new
        l_scr[...] = alpha * l_scr[...] + jnp.sum(p, axis=1, keepdims=True)
        acc_scr[...] = alpha * acc_scr[...] + _dot(p, v)

    def rows_of(x):
        return jnp.concatenate([x] * NSA_REP, axis=0)

    @pl.when(j == 0)
    def _():
        m_scr[...] = jnp.full_like(m_scr, NEG_INF)
        l_scr[...] = jnp.zeros_like(l_scr)
        acc_scr[...] = jnp.zeros_like(acc_scr)
        n_new = snew_ref.shape[1]
        d_new = q_idx - lax.broadcasted_iota(jnp.int32, (1, n_new), 1)
        chosen = rows_of(selnew_ref[0])[:, 0:1] > 0.5
        online(_dot_nt(qbd, snew_ref[0, :, 0:KV_W]) + bsn_ref[...], chosen & (d_new >= 0),
               snew_ref[0, :, KV_W:2 * KV_W])

    selk = rows_of(jnp.dot(sel_ref[0, 0].astype(BF16), ex_ref[...], preferred_element_type=F32))
    online(_dot_nt(qbd, buf[slot, 0]) + bs_ref[0], selk > 0.5, buf[slot, 1])

    @pl.when(j == n_chunk - 1)
    def _():
        o_s = acc_scr[...] / jnp.maximum(l_scr[...], TINY)
        g = _sigmoid(gate_ref[0])
        o_ref[0] = g[:, 0:1] * oc_ref[0] + g[:, 1:2] * o_s + g[:, 2:3] * ow_ref[0]


def _nsa_sample(layer, q_nsa, nsa_rows, win_rows, gates, kcvc, state_win, cache_nsa, page_table, rel_bias,
                chunk_pages):
    bs, ts, _ = q_nsa.shape
    n_pages = page_table.shape[1]
    page_rows = cache_nsa.shape[2]
    past = n_pages * page_rows
    n_c = kcvc.shape[1]
    w_buf = state_win.shape[2]
    n_slc = -(-(past + ts) // SLC_BLOCK)
    n_top = min(SLC_TOP, n_slc)
    n_past_blk = past // SLC_BLOCK
    nb_pad = -(-n_slc // LANES) * LANES
    g_, r_ = NSA_KV, NSA_REP
    rows = r_ * g_ * ts
    new_pad = LANES
    assert past % SLC_BLOCK == 0 and ts <= SLC_BLOCK and w_buf == WINDOW

    q5 = q_nsa.reshape(bs, ts, g_, r_, HEAD_DIM)
    qbd = jnp.einsum("bqgrd,gk->brgqkd", q5, jnp.eye(g_, dtype=F32)).reshape(bs, rows, KV_W)
    qq = jnp.arange(ts)

    def row_bias(dist_qk):
        bias = _bias_at(rel_bias, dist_qk).reshape(g_, r_, ts, -1)
        return bias.transpose(1, 0, 2, 3).reshape(rows, -1)

    bias_c = row_bias(past + qq[:, None] - (jnp.arange(n_c)[None, :] * CMP_STRIDE + CMP_LEN - 1))
    bias_wp = row_bias(w_buf + qq[:, None] - jnp.arange(w_buf)[None, :])
    bias_new = row_bias(qq[:, None] - jnp.arange(new_pad)[None, :])
    chunk_keys = chunk_pages * page_rows
    n_chunk = n_pages // chunk_pages
    bias_s = row_bias(past + qq[:, None] - jnp.arange(past)[None, :])
    bias_s = bias_s.reshape(rows, n_chunk, chunk_keys).transpose(1, 0, 2)
    ovl = _overlap(n_c, n_slc, nb_pad)
    wnew = jnp.pad(win_rows, ((0, 0), (0, new_pad - ts), (0, 0)))
    snew = jnp.pad(nsa_rows[:, :, 2 * KV_W:4 * KV_W], ((0, 0), (0, new_pad - ts), (0, 0)))

    full = lambda a: pl.BlockSpec(a.shape, lambda b: (0,) * a.ndim)
    oc, ow, sel = pl.pallas_call(
        functools.partial(_nsa_a_kernel, ts=ts, past=past, n_slc=n_slc, n_top=n_top),
        out_shape=[SDS((bs, rows, KV_W), F32), SDS((bs, rows, KV_W), F32), SDS((bs, g_ * ts, nb_pad), F32)],
        grid=(bs,),
        in_specs=[pl.BlockSpec((1, rows, KV_W), lambda b: (b, 0, 0)),
                  pl.BlockSpec((1, n_c, 2 * KV_W), lambda b: (b, 0, 0)),
                  pl.BlockSpec((1, 1, w_buf, 2 * KV_W), lambda b: (layer, b, 0, 0)),
                  pl.BlockSpec((1, new_pad, 2 * KV_W), lambda b: (b, 0, 0)),
                  full(bias_c), full(bias_wp), full(bias_new), full(ovl)],
        out_specs=[pl.BlockSpec((1, rows, KV_W), lambda b: (b, 0, 0)),
                   pl.BlockSpec((1, rows, KV_W), lambda b: (b, 0, 0)),
                   pl.BlockSpec((1, g_ * ts, nb_pad), lambda b: (b, 0, 0))],
        compiler_params=_cparams(("arbitrary",)), name="nsa_sample_a",
    )(qbd, kcvc, state_win, wnew, bias_c, bias_wp, bias_new, ovl)

    blk_per_chunk = chunk_keys // SLC_BLOCK
    sel_chunks = sel[:, :, :n_past_blk].reshape(bs, g_ * ts, n_chunk, blk_per_chunk).transpose(0, 2, 1, 3)
    sel_chunks = jnp.pad(sel_chunks, ((0, 0), (0, 0), (0, 0), (0, LANES - blk_per_chunk)))
    sel_new = jnp.pad(sel[:, :, n_past_blk:n_past_blk + 1], ((0, 0), (0, 0), (0, LANES - 1)))
    ex = _expand(LANES, chunk_keys)
    gt = gates[:, :, :GATE_W].reshape(bs, ts, g_, r_, 3).transpose(0, 3, 2, 1, 4).reshape(bs, rows, 3)
    gt = jnp.pad(gt, ((0, 0), (0, 0), (0, LANES - 3)))

    kern = functools.partial(_nsa_b_kernel, layer=layer, n_seq=bs, n_chunk=n_chunk, chunk_pages=chunk_pages,
                             page_rows=page_rows, ts=ts)
    per_seq = lambda a: pl.BlockSpec((1,) + a.shape[1:], lambda b, j, pt: (b,) + (0,) * (a.ndim - 1))
    const = lambda a: pl.BlockSpec(a.shape, lambda b, j, pt: (0,) * a.ndim)
    grid_spec = pltpu.PrefetchScalarGridSpec(
        num_scalar_prefetch=1, grid=(bs, n_chunk),
        in_specs=[per_seq(qbd), per_seq(snew),
                  pl.BlockSpec((1, 1, g_ * ts, LANES), lambda b, j, pt: (b, j, 0, 0)),
                  per_seq(sel_new),
                  pl.BlockSpec((1, rows, chunk_keys), lambda b, j, pt: (j, 0, 0)),
                  const(bias_new), const(ex), per_seq(oc), per_seq(ow), per_seq(gt),
                  pl.BlockSpec(memory_space=pl.ANY)],
        out_specs=pl.BlockSpec((1, rows, KV_W), lambda b, j, pt: (b, 0, 0)),
        scratch_shapes=[pltpu.VMEM((2, 2, chunk_keys, KV_W), F32),
                        pltpu.SemaphoreType.DMA((2,)),
                        pltpu.VMEM((rows, 1), F32), pltpu.VMEM((rows, 1), F32), pltpu.VMEM((rows, KV_W), F32)])
    o = pl.pallas_call(
        kern, out_shape=SDS((bs, rows, KV_W), F32), grid_spec=grid_spec,
        compiler_params=_cparams(("arbitrary", "arbitrary")), name="nsa_sample_b",
    )(page_table, qbd, snew, sel_chunks, sel_new, bias_s, bias_new, ex, oc, ow, gt, cache_nsa)
    o = o.reshape(bs, r_, g_, ts, g_, HEAD_DIM)
    o = jnp.stack([o[:, :, g, :, g, :] for g in range(g_)], axis=2)
    return o.transpose(0, 3, 2, 1, 4).reshape(bs, ts, NSA_W)


def _layer_weights(l, w_in, w_out, w_up, w_down, cmp_pe, cmp_w1, cmp_w2):
    w_in_bf = jnp.pad(w_in[l], ((0, 0), (0, D_IN_PAD - D_IN))).astype(BF16)
    return dict(w_in=w_in_bf, w_out=w_out[l].astype(BF16), w_up=w_up[l].astype(BF16),
                w_down=w_down[l].astype(BF16), cmp=_compress_weights(cmp_pe[l], cmp_w1[l], cmp_w2[l]))


def _dense_tail(o_sb, o_nsa, x2d, mod, lw, g_post_mix, g_pre_ffn, g_post_ffn, tm, tf):
    x1 = _out_proj(o_sb, o_nsa, x2d, g_post_mix, mod, lw["w_out"], tm)
    return _ffn(x1, g_pre_ffn, g_post_ffn, mod, lw["w_up"], lw["w_down"], tm, tf)


def kernel(x_prompt, x_sample, c_prompt, c_sample, cache_sb_kv, cache_nsa_kv, state_nsa_win, page_table, w_ada, b_ada, g_pre_mix, g_post_mix, g_pre_ffn, g_post_ffn, w_in, w_out, cmp_pe, cmp_w1, cmp_w2, rel_bias, w_up, w_down):
    bp, t, d = x_prompt.shape
    bs, ts, _ = x_sample.shape
    depth = w_in.shape[0]
    n_phys, page_rows = cache_sb_kv.shape[1], cache_sb_kv.shape[2]
    n_pages = page_table.shape[1]
    w_buf = state_nsa_win.shape[2]
    f = w_up.shape[2]
    page_table = page_table.astype(jnp.int32)

    cache_sb = cache_sb_kv.reshape(depth, n_phys, page_rows, 2 * SB_W)
    cache_nsa = cache_nsa_kv.reshape(depth, n_phys, page_rows, 4 * KV_W)
    state_win = state_nsa_win.reshape(depth, bs, w_buf, 2 * KV_W)

    tm_p = 256
    tm_ffn_p = 512 if t % 512 == 0 else tm_p
    assert t % tm_p == 0
    tf = 1024 if f % 1024 == 0 else f
    tq_sb = 256
    sb_chunk_pages = math.gcd(n_pages, 16)
    cmp_chunk_pages = math.gcd(n_pages, 64)
    slc_chunk_pages = math.gcd(n_pages, 16)

    mod = _ada(jnp.concatenate([c_prompt, c_sample], axis=0), w_ada, b_ada)

    y_p = x_prompt.reshape(bp * t, d)
    y_s = x_sample.reshape(bs * ts, d)
    outs = {k: [] for k in ("sb_p", "sb_s", "nsa_p", "nsa_s", "win_p", "win_s")}
    for l in range(depth):
        lw = _layer_weights(l, w_in, w_out, w_up, w_down, cmp_pe, cmp_w1, cmp_w2)

        mod_p = _Mod(mod[l, :bp].reshape(bp, 1, 6 * d), t)
        q_sb, sb_rows, q_nsa, nsa_rows, win_rows, gates = _in_proj(y_p, g_pre_mix[l], mod_p, lw["w_in"], tm_p)
        o_sb = _sb_prompt(q_sb.reshape(bp, t, SB_W), sb_rows.reshape(bp, t, 2 * SB_W), tq_sb)
        nsa3 = nsa_rows.reshape(bp, t, 4 * KV_W)
        win3 = win_rows.reshape(bp, t, 2 * KV_W)
        kcvc = _cmp_prompt(nsa3, lw["cmp"])
        o_nsa = _nsa_prompt(q_nsa.reshape(bp, t, NSA_W), nsa3, win3, gates.reshape(bp, t, LANES), kcvc,
                            rel_bias, 128, 512 if t % 512 == 0 else 256)
        y_p = _dense_tail(o_sb.reshape(bp * t, SB_W), o_nsa.reshape(bp * t, NSA_W), y_p, mod_p, lw,
                          g_post_mix[l], g_pre_ffn[l], g_post_ffn[l], tm_ffn_p, tf)
        outs["sb_p"].append(sb_rows.reshape(bp, t, 2, SB_HEADS, HEAD_DIM))
        outs["nsa_p"].append(nsa_rows.reshape(bp, t, 4, NSA_KV, HEAD_DIM))
        wp = win3 if t >= w_buf else jnp.pad(win3, ((0, 0), (w_buf - t, 0), (0, 0)))
        outs["win_p"].append(wp[:, wp.shape[1] - w_buf:].reshape(bp, w_buf, 2, NSA_KV, HEAD_DIM))

        tm_s = bs * ts
        mod_s = _Mod(jnp.repeat(mod[l, bp:], ts, axis=0).reshape(1, bs * ts, 6 * d), 1)
        q_sb, sb_rows, q_nsa, nsa_rows, win_rows, gates = _in_proj(y_s, g_pre_mix[l], mod_s, lw["w_in"], tm_s)
        sb3 = sb_rows.reshape(bs, ts, 2 * SB_W)
        nsa3 = nsa_rows.reshape(bs, ts, 4 * KV_W)
        win3 = win_rows.reshape(bs, ts, 2 * KV_W)
        o_sb = _sb_sample(l, q_sb.reshape(bs, ts, SB_W), sb3, cache_sb, page_table, sb_chunk_pages, 256)
        kcvc = _cmp_sample(l, cache_nsa, page_table, lw["cmp"], cmp_chunk_pages)
        o_nsa = _nsa_sample(l, q_nsa.reshape(bs, ts, NSA_W), nsa3, win3, gates.reshape(bs, ts, LANES), kcvc,
                            state_win, cache_nsa, page_table, rel_bias, slc_chunk_pages)
        y_s = _dense_tail(o_sb.reshape(bs * ts, SB_W), o_nsa.reshape(bs * ts, NSA_W), y_s, mod_s, lw,
                          g_post_mix[l], g_pre_ffn[l], g_post_ffn[l], tm_s, tf)
        outs["sb_s"].append(sb_rows.reshape(bs, ts, 2, SB_HEADS, HEAD_DIM))
        outs["nsa_s"].append(nsa_rows.reshape(bs, ts, 4, NSA_KV, HEAD_DIM))
        win_all = jnp.concatenate([state_win[l], win3], axis=1)
        outs["win_s"].append(win_all[:, win_all.shape[1] - w_buf:].reshape(bs, w_buf, 2, NSA_KV, HEAD_DIM))

    return (y_p.reshape(bp, t, d), y_s.reshape(bs, ts, d), jnp.stack(outs["sb_p"]), jnp.stack(outs["sb_s"]),
            jnp.stack(outs["nsa_p"]), jnp.stack(outs["nsa_s"]), jnp.stack(outs["win_p"]),
            jnp.stack(outs["win_s"]))
```

```python
import functools
import math

import jax
import jax.numpy as jnp
import numpy as np
from jax import lax
from jax.experimental import pallas as pl
from jax.experimental.pallas import tpu as pltpu

F32 = jnp.float32
BF16 = jnp.bfloat16
SDS = jax.ShapeDtypeStruct

HEAD_DIM = 64
SB_HEADS = 8
NSA_KV = 2
NSA_REP = 4
NSA_HEADS = NSA_KV * NSA_REP
SB_W = SB_HEADS * HEAD_DIM
NSA_W = NSA_HEADS * HEAD_DIM
KV_W = NSA_KV * HEAD_DIM
GATE_W = 3 * NSA_HEADS
D_IN = 3 * SB_W + NSA_W + 6 * KV_W + GATE_W
CMP_STRIDE = 16
CMP_LEN = 2 * CMP_STRIDE
SLC_BLOCK = 64
SLC_TOP = 16
WINDOW = 512
N_BUCKETS = 32
MAX_DISTANCE = 1024
RMS_EPS = 1e-6
NEG_INF = -1e30
FORCE = 1e9
TINY = 1e-30
SCALE = HEAD_DIM ** -0.5

LANES = 128
VMEM_LIMIT = 52 * 1024 * 1024


def _cparams(sem):
    return pltpu.CompilerParams(dimension_semantics=sem, vmem_limit_bytes=VMEM_LIMIT)


def _dot(a, b):
    return jnp.dot(a.astype(BF16), b.astype(BF16), preferred_element_type=F32)


def _dot_nt(a, b):
    return lax.dot_general(a.astype(BF16), b.astype(BF16), (((1,), (1,)), ((), ())),
                           preferred_element_type=F32)


def _split(a):
    hi = a.astype(BF16)
    lo = (a - hi.astype(F32)).astype(BF16)
    return hi, lo


def _dot_hl(a, b_bf16):
    hi, lo = _split(a)
    return (jnp.dot(hi, b_bf16, preferred_element_type=F32)
            + jnp.dot(lo, b_bf16, preferred_element_type=F32))


def _dot3(a, b):
    a_hi, a_lo = _split(a)
    b_hi, b_lo = _split(b)
    return (jnp.dot(a_hi, b_hi, preferred_element_type=F32)
            + jnp.dot(a_hi, b_lo, preferred_element_type=F32)
            + jnp.dot(a_lo, b_hi, preferred_element_type=F32))


def _sigmoid(x):
    return 1.0 / (1.0 + jnp.exp(-x))


def _rms(x, g):
    return x * lax.rsqrt(jnp.mean(x * x, axis=-1, keepdims=True) + RMS_EPS) * g


def _masked_softmax(lg, mask):
    lg = jnp.where(mask, lg, NEG_INF)
    m = jnp.max(lg, axis=-1, keepdims=True)
    e = jnp.where(mask, jnp.exp(lg - m), 0.0)
    return e / jnp.maximum(jnp.sum(e, axis=-1, keepdims=True), TINY)


def _stick_terms(z):
    sp = jnp.log(1.0 + jnp.exp(-jnp.abs(z)))
    lb = jnp.minimum(z, 0.0) - sp
    return lb, lb - z


def _gelu_tanh(x):
    return 0.5 * x * (1.0 + jnp.tanh(math.sqrt(2.0 / math.pi) * (x + 0.044715 * (x * x * x))))


def _ada_kernel(c_ref, w_ref, b_ref, o_ref):
    c = c_ref[...]
    o_ref[0] = _dot3(c * _sigmoid(c), w_ref[0]) + b_ref[0]


def _ada(c_all, w_ada, b_ada):
    depth, d, d6 = w_ada.shape
    nb = c_all.shape[0]
    tn = d6 // 4
    return pl.pallas_call(
        _ada_kernel, out_shape=SDS((depth, nb, d6), F32), grid=(depth, d6 // tn),
        in_specs=[pl.BlockSpec((nb, d), lambda l, j: (0, 0)),
                  pl.BlockSpec((1, d, tn), lambda l, j: (l, 0, j)),
                  pl.BlockSpec((1, 1, tn), lambda l, j: (l, 0, j))],
        out_specs=pl.BlockSpec((1, nb, tn), lambda l, j: (l, 0, j)),
        compiler_params=_cparams(("arbitrary", "arbitrary")), name="ada",
    )(c_all, w_ada, b_ada.reshape(depth, 1, d6))


def _mod_spec(mod3, d, tm, chunk):
    nb, rb, _ = mod3.shape
    if rb == 1:
        rows_per_b = mod3.rows_per_b
        return pl.BlockSpec((1, 1, d), lambda i, *_: ((i * tm) // rows_per_b, 0, chunk))
    return pl.BlockSpec((1, tm, d), lambda i, *_: (0, i, chunk))


class _Mod:
    def __init__(self, arr, rows_per_b):
        self.arr = arr
        self.shape = arr.shape
        self.rows_per_b = rows_per_b


_PROJ_SPLITS = (("q_sb", 0, SB_W), ("sb_rows", SB_W, 3 * SB_W), ("q_nsa", 3 * SB_W, 3 * SB_W + NSA_W),
                ("nsa_rows", 3 * SB_W + NSA_W, 3 * SB_W + NSA_W + 4 * KV_W),
                ("win_rows", 3 * SB_W + NSA_W + 4 * KV_W, 3 * SB_W + NSA_W + 6 * KV_W),
                ("gates", 3 * SB_W + NSA_W + 6 * KV_W, 3 * SB_W + NSA_W + 6 * KV_W + LANES))
D_IN_PAD = _PROJ_SPLITS[-1][2]


def _in_proj_kernel(x_ref, g_ref, sc_ref, sh_ref, w_ref, *out_refs):
    h = _rms(x_ref[...], g_ref[...]) * (1.0 + sc_ref[0]) + sh_ref[0]
    p = jnp.dot(h.astype(BF16), w_ref[...], preferred_element_type=F32)
    for o_ref, (_, lo, hi) in zip(out_refs, _PROJ_SPLITS):
        o_ref[...] = p[:, lo:hi]


def _in_proj(x2d, g, mod, w_bf, tm):
    r, d = x2d.shape
    widths = [hi - lo for _, lo, hi in _PROJ_SPLITS]
    return pl.pallas_call(
        _in_proj_kernel, out_shape=[SDS((r, w), F32) for w in widths], grid=(r // tm,),
        in_specs=[pl.BlockSpec((tm, d), lambda i: (i, 0)),
                  pl.BlockSpec((1, d), lambda i: (0, 0)),
                  _mod_spec(mod, d, tm, 1), _mod_spec(mod, d, tm, 0),
                  pl.BlockSpec((d, D_IN_PAD), lambda i: (0, 0))],
        out_specs=[pl.BlockSpec((tm, w), lambda i: (i, 0)) for w in widths],
        compiler_params=_cparams(("arbitrary",)), name="in_proj",
    )(x2d, g.reshape(1, d), mod.arr, mod.arr, w_bf)


def _out_proj_kernel(osb_ref, onsa_ref, x_ref, g_ref, gt_ref, w_ref, o_ref):
    mix = (jnp.dot(osb_ref[...].astype(BF16), w_ref[0:SB_W, :], preferred_element_type=F32)
           + jnp.dot(onsa_ref[...].astype(BF16), w_ref[SB_W:SB_W + NSA_W, :], preferred_element_type=F32))
    o_ref[...] = x_ref[...] + gt_ref[0] * _rms(mix, g_ref[...])


def _out_proj(o_sb, o_nsa, x2d, g, mod, w_bf, tm):
    r, d = x2d.shape
    return pl.pallas_call(
        _out_proj_kernel, out_shape=SDS((r, d), F32), grid=(r // tm,),
        in_specs=[pl.BlockSpec((tm, SB_W), lambda i: (i, 0)),
                  pl.BlockSpec((tm, NSA_W), lambda i: (i, 0)),
                  pl.BlockSpec((tm, d), lambda i: (i, 0)),
                  pl.BlockSpec((1, d), lambda i: (0, 0)),
                  _mod_spec(mod, d, tm, 2),
                  pl.BlockSpec((SB_W + NSA_W, d), lambda i: (0, 0))],
        out_specs=pl.BlockSpec((tm, d), lambda i: (i, 0)),
        compiler_params=_cparams(("arbitrary",)), name="out_proj",
    )(o_sb, o_nsa, x2d, g.reshape(1, d), mod.arr, w_bf)


def _ffn_kernel(x_ref, gpre_ref, sc_ref, sh_ref, wup_ref, wdn_ref, gpost_ref, gt_ref, o_ref, h_scr, acc_scr):
    k = pl.program_id(1)

    @pl.when(k == 0)
    def _():
        h = _rms(x_ref[...], gpre_ref[...]) * (1.0 + sc_ref[0]) + sh_ref[0]
        h_scr[...] = h.astype(BF16)
        acc_scr[...] = jnp.zeros_like(acc_scr)

    u = jnp.dot(h_scr[...], wup_ref[...], preferred_element_type=F32)
    a = jnp.square(jnp.maximum(u, 0.0))
    acc_scr[...] += jnp.dot(a.astype(BF16), wdn_ref[...], preferred_element_type=F32)

    @pl.when(k == pl.num_programs(1) - 1)
    def _():
        o_ref[...] = x_ref[...] + gt_ref[0] * _rms(acc_scr[...], gpost_ref[...])


def _ffn(x2d, g_pre, g_post, mod, wup_bf, wdn_bf, tm, tf):
    r, d = x2d.shape
    f = wup_bf.shape[1]
    return pl.pallas_call(
        _ffn_kernel, out_shape=SDS((r, d), F32), grid=(r // tm, f // tf),
        in_specs=[pl.BlockSpec((tm, d), lambda i, k: (i, 0)),
                  pl.BlockSpec((1, d), lambda i, k: (0, 0)),
                  _mod_spec(mod, d, tm, 4), _mod_spec(mod, d, tm, 3),
                  pl.BlockSpec((d, tf), lambda i, k: (0, k)),
                  pl.BlockSpec((tf, d), lambda i, k: (k, 0)),
                  pl.BlockSpec((1, d), lambda i, k: (0, 0)),
                  _mod_spec(mod, d, tm, 5)],
        out_specs=pl.BlockSpec((tm, d), lambda i, k: (i, 0)),
        scratch_shapes=[pltpu.VMEM((tm, d), BF16), pltpu.VMEM((tm, d), F32)],
        compiler_params=_cparams(("arbitrary", "arbitrary")), name="ffn",
    )(x2d, g_pre.reshape(1, d), mod.arr, mod.arr, wup_bf, wdn_bf, g_post.reshape(1, d), mod.arr)


def _tri(n):
    j = np.arange(n)
    return jnp.asarray((j[:, None] > j[None, :]).astype(np.float32), dtype=BF16)


def _sb_prompt_kernel(q_ref, k_ref, v_ref, tri_ref, o_ref, *, tq):
    qi = pl.program_id(2)
    q2 = q_ref[0] * SCALE
    lane = lax.broadcasted_iota(jnp.int32, (tq, LANES), 1)
    qm = jnp.concatenate([jnp.where(lane < HEAD_DIM, q2, 0.0), jnp.where(lane >= HEAD_DIM, q2, 0.0)],
                         axis=0).astype(BF16)
    row = lax.broadcasted_iota(jnp.int32, (2 * tq, tq), 0)
    col = lax.broadcasted_iota(jnp.int32, (2 * tq, tq), 1)
    causal = col < jnp.where(row >= tq, row - tq, row)
    tri = tri_ref[...]

    def tile(c, carry, acc, diagonal):
        k_c = k_ref[0, pl.ds(pl.multiple_of(c * tq, tq), tq), :]
        v_c = v_ref[0, pl.ds(pl.multiple_of(c * tq, tq), tq), :]
        lb, lk = _stick_terms(_dot_nt(qm, k_c))
        if diagonal:
            lk = jnp.where(causal, lk, 0.0)
        tot = _dot_hl(lk, tri) + carry
        att = jnp.exp(lb + tot)
        if diagonal:
            att = jnp.where(causal, att, 0.0)
        acc = acc + _dot(att, v_c)
        carry = carry + jnp.sum(lk, axis=1, keepdims=True)
        return carry, acc

    carry, acc = tile(qi, jnp.zeros((2 * tq, 1), F32), jnp.zeros((2 * tq, LANES), F32), True)
    carry, acc = lax.fori_loop(
        0, qi, lambda i, ca: tile(qi - 1 - i, ca[0], ca[1], False), (carry, acc))
    o_ref[0] = jnp.where(lane < HEAD_DIM, acc[0:tq], acc[tq:2 * tq])


def _sb_prompt(q_sb, sb_rows, tq):
    b, t, _ = q_sb.shape
    npair = SB_W // LANES
    return pl.pallas_call(
        functools.partial(_sb_prompt_kernel, tq=tq), out_shape=SDS((b, t, SB_W), F32),
        grid=(b, npair, t // tq),
        in_specs=[pl.BlockSpec((1, tq, LANES), lambda bi, hp, qi: (bi, qi, hp)),
                  pl.BlockSpec((1, t, LANES), lambda bi, hp, qi: (bi, 0, hp)),
                  pl.BlockSpec((1, t, LANES), lambda bi, hp, qi: (bi, 0, npair + hp)),
                  pl.BlockSpec((tq, tq), lambda bi, hp, qi: (0, 0))],
        out_specs=pl.BlockSpec((1, tq, LANES), lambda bi, hp, qi: (bi, qi, hp)),
        compiler_params=_cparams(("arbitrary", "arbitrary", "arbitrary")), name="sb_prompt",
    )(q_sb, sb_rows, sb_rows, _tri(tq))


def _compress_weights(cmp_pe, cmp_w1, cmp_w2):
    w1r = cmp_w1.reshape(2, CMP_LEN, HEAD_DIM, HEAD_DIM)
    lo, hi = w1r[:, :CMP_STRIDE], w1r[:, CMP_STRIDE:]
    z = jnp.zeros_like(lo)
    top = jnp.concatenate([lo, z, hi, z], axis=-1)
    bot = jnp.concatenate([z, lo, z, hi], axis=-1)
    w1p = jnp.concatenate([top, bot], axis=2).astype(BF16)
    z2 = jnp.zeros_like(cmp_w2)
    w2p = jnp.concatenate([jnp.concatenate([cmp_w2, z2], axis=-1),
                           jnp.concatenate([z2, cmp_w2], axis=-1)], axis=1).astype(BF16)
    pe = jnp.broadcast_to(cmp_pe.reshape(2, 1, CMP_LEN * HEAD_DIM), (2, 8, CMP_LEN * HEAD_DIM))
    w1cat = jnp.concatenate([cmp_w1, cmp_w1], axis=-1)
    return w1p, w2p, pe, w1cat


def _compress_partial(slab_ref, w1p_ref, kind, n):
    u = jnp.zeros((n, 2 * LANES), F32)
    for p in range(CMP_STRIDE):
        x = slab_ref[pl.ds(p, n, stride=CMP_STRIDE), :]
        u = u + jnp.dot(x.astype(BF16), w1p_ref[kind, p], preferred_element_type=F32)
    return u


def _compress_finish(u, kind, w2p_ref, pe_ref, w1cat_ref):
    n = u.shape[0]
    pe_term = _dot3(pe_ref[kind], w1cat_ref[kind])[0:1]
    nxt = pltpu.roll(u[:, LANES:], n - 1, 0)
    h = _gelu_tanh(u[:, :LANES] + nxt + pe_term)
    return jnp.dot(h.astype(BF16), w2p_ref[kind], preferred_element_type=F32)


def _cmp_prompt_kernel(k_ref, v_ref, w1p_ref, w2p_ref, pe_ref, w1cat_ref, o_ref, *, n):
    for kind, ref in enumerate((k_ref, v_ref)):
        u = _compress_partial(ref.at[0], w1p_ref, kind, n)
        o_ref[0, :, kind * LANES:(kind + 1) * LANES] = _compress_finish(u, kind, w2p_ref, pe_ref, w1cat_ref)


def _cmp_prompt(nsa_rows, cw):
    b, t, _ = nsa_rows.shape
    n = t // CMP_STRIDE
    w1p, w2p, pe, w1cat = cw
    full = lambda a: pl.BlockSpec(a.shape, lambda bi: (0,) * a.ndim)
    return pl.pallas_call(
        functools.partial(_cmp_prompt_kernel, n=n), out_shape=SDS((b, n, 2 * LANES), F32), grid=(b,),
        in_specs=[pl.BlockSpec((1, t, LANES), lambda bi: (bi, 0, 0)),
                  pl.BlockSpec((1, t, LANES), lambda bi: (bi, 0, 1)),
                  full(w1p), full(w2p), full(pe), full(w1cat)],
        out_specs=pl.BlockSpec((1, n, 2 * LANES), lambda bi: (bi, 0, 0)),
        compiler_params=_cparams(("arbitrary",)), name="cmp_prompt",
    )(nsa_rows, nsa_rows, w1p, w2p, pe, w1cat)


def _t5_bucket(dist):
    dist = jnp.maximum(dist, 0)
    exact = N_BUCKETS // 2
    ratio = jnp.log(jnp.maximum(dist, 1).astype(F32) / exact) / math.log(MAX_DISTANCE / exact)
    large = jnp.minimum(exact + (ratio * (N_BUCKETS - exact)).astype(jnp.int32), N_BUCKETS - 1)
    return jnp.where(dist < exact, dist, large)


def _bias_at(rel_bias, dist):
    onehot = (_t5_bucket(dist)[..., None] == jnp.arange(N_BUCKETS)).astype(F32)
    return jnp.einsum("...k,kh->h...", onehot, rel_bias.astype(F32), precision=lax.Precision.HIGHEST)


def _overlap(n_c, n_slc, n_slc_pad):
    c = np.arange(n_c)
    j = np.arange(n_slc_pad)
    c_start, c_end = c * CMP_STRIDE, c * CMP_STRIDE + CMP_LEN - 1
    s_start = j * SLC_BLOCK
    ov = ((c_end[:, None] >= s_start[None, :]) & (c_start[:, None] <= s_start[None, :] + SLC_BLOCK - 1)
          & (c[:, None] < n_c - 1) & (j[None, :] < n_slc))
    return jnp.asarray(ov.astype(np.float32), dtype=BF16)


def _expand(n_blk_pad, n_keys, first_block=0):
    blk = np.arange(n_blk_pad)
    s = np.arange(n_keys)
    return jnp.asarray((s[None, :] // SLC_BLOCK == blk[:, None] + first_block).astype(np.float32), dtype=BF16)


def _gate_expand():
    e = np.zeros((NSA_KV, LANES, 3 * NSA_REP * HEAD_DIM), np.float32)
    for g in range(NSA_KV):
        for r in range(NSA_REP):
            for j in range(3):
                e[g, (g * NSA_REP + r) * 3 + j, j * NSA_REP * HEAD_DIM + r * HEAD_DIM:][:HEAD_DIM] = 1.0
    return jnp.asarray(e, dtype=BF16)


def _top_blocks(imp, cur, n_slc, n_top):
    tq = imp.shape[0]
    blk = lax.broadcasted_iota(jnp.int32, (tq, LANES), 1)
    valid = blk <= cur
    forced = valid & ((blk == 0) | (blk >= cur - 1))
    score = jnp.where(forced, FORCE, jnp.where(valid, imp, -FORCE))
    rank = jnp.zeros((tq, LANES), F32)
    for i in range(n_slc):
        si = score[:, i:i + 1]
        beats = (si > score) | ((si == score) & (blk > i))
        rank = rank + jnp.where(beats, 1.0, 0.0)
    return jnp.where((rank < n_top) & valid, 1.0, 0.0)


def _nsa_prompt_kernel(q_ref, kc_ref, vc_ref, ks_ref, vs_ref, kw_ref, vw_ref, gate_ref, ge_ref, bc_ref,
                       strip_ref, ovl_ref, ex_ref, o_ref, *, tq, tk, n_c, n_slc, n_top, m_off):
    qi = pl.program_id(2)
    q0 = qi * tq
    r4 = NSA_REP * tq
    q = (q_ref[0, 0] * SCALE).reshape(r4, HEAD_DIM).astype(BF16)
    t_pos = q0 + lax.broadcasted_iota(jnp.int32, (tq, 1), 0)

    c_idx = lax.broadcasted_iota(jnp.int32, (1, n_c), 1)
    mc = ((CMP_STRIDE * c_idx + CMP_LEN - 1) <= t_pos) & (c_idx < n_c - 1)
    lc = _dot_nt(q, kc_ref[0, 0]).reshape(NSA_REP, tq, n_c) + bc_ref[0]
    pc = _masked_softmax(lc, mc[None])
    oc = _dot(pc.reshape(r4, n_c), vc_ref[0, 0])

    imp = _dot_hl(jnp.sum(pc, axis=0), ovl_ref[...])
    sel = _top_blocks(imp, t_pos >> 6, n_slc, n_top).astype(BF16)

    def slc_tile(c, carry):
        m, l, acc = carry
        s0 = pl.multiple_of(c * tk, tk)
        ls = _dot_nt(q, ks_ref[0, 0, pl.ds(s0, tk), :]).reshape(NSA_REP, tq, tk)
        ls = ls + strip_ref[0, :, :, pl.ds(pl.multiple_of(s0 - q0 + m_off, LANES), tk)]
        selk = jnp.dot(sel, ex_ref[:, pl.ds(s0, tk)], preferred_element_type=F32)
        s_pos = s0 + lax.broadcasted_iota(jnp.int32, (1, tk), 1)
        mk = ((selk > 0.5) & (s_pos <= t_pos))[None]
        lg = jnp.where(mk, ls, NEG_INF)
        m_new = jnp.maximum(m, jnp.max(lg, axis=-1, keepdims=True))
        alpha = jnp.exp(m - m_new)
        p = jnp.where(mk, jnp.exp(lg - m_new), 0.0)
        l = alpha * l + jnp.sum(p, axis=-1, keepdims=True)
        pv = _dot(p.reshape(r4, tk), vs_ref[0, 0, pl.ds(s0, tk), :]).reshape(NSA_REP, tq, HEAD_DIM)
        return m_new, l, alpha * acc + pv

    n_tiles = (q0 + tq + tk - 1) // tk
    m, l, acc = lax.fori_loop(0, n_tiles, slc_tile,
                              (jnp.full((NSA_REP, tq, 1), NEG_INF, F32), jnp.zeros((NSA_REP, tq, 1), F32),
                               jnp.zeros((NSA_REP, tq, HEAD_DIM), F32)))
    o_s = (acc / jnp.maximum(l, TINY)).reshape(r4, HEAD_DIM)

    band = WINDOW + tq
    w0 = pl.multiple_of(jnp.maximum(q0 - WINDOW, 0), LANES)
    lw = _dot_nt(q, kw_ref[0, 0, pl.ds(w0, band), :]).reshape(NSA_REP, tq, band)
    lw = lw + strip_ref[0, :, :, pl.ds(pl.multiple_of(w0 - q0 + m_off, LANES), band)]
    dist = t_pos - (w0 + lax.broadcasted_iota(jnp.int32, (1, band), 1))
    pw = _masked_softmax(lw, ((dist >= 0) & (dist < WINDOW))[None])
    o_w = _dot(pw.reshape(r4, band), vw_ref[0, 0, pl.ds(w0, band), :])

    gexp = _dot_hl(_sigmoid(gate_ref[0]), ge_ref[0])
    gw = NSA_REP * HEAD_DIM

    def token_major(o):
        return jnp.concatenate([o[r * tq:(r + 1) * tq] for r in range(NSA_REP)], axis=1)

    o_ref[0] = (gexp[:, 0:gw] * token_major(oc) + gexp[:, gw:2 * gw] * token_major(o_s)
                + gexp[:, 2 * gw:3 * gw] * token_major(o_w))


def _nsa_prompt(q_nsa, nsa_rows, win_rows, gates, kcvc, rel_bias, tq, tk):
    b, t, _ = q_nsa.shape
    n_c = t // CMP_STRIDE
    n_slc = -(-t // SLC_BLOCK)
    n_top = min(SLC_TOP, n_slc)
    assert n_slc <= LANES and t % tk == 0 and t >= WINDOW + tq
    g_, r_ = NSA_KV, NSA_REP
    qh = q_nsa.reshape(b, t, g_, r_, HEAD_DIM).transpose(0, 2, 3, 1, 4)
    nsah = nsa_rows.reshape(b, t, 4, g_, HEAD_DIM).transpose(2, 0, 3, 1, 4)
    winh = win_rows.reshape(b, t, 2, g_, HEAD_DIM).transpose(2, 0, 3, 1, 4)
    kch = kcvc.reshape(b, n_c, 2, g_, HEAD_DIM).transpose(2, 0, 3, 1, 4)
    m_off = t - tq
    width = m_off + max(tk, WINDOW + tq)
    tt = jnp.arange(t)
    bias_c = _bias_at(rel_bias, tt[:, None] - (jnp.arange(n_c)[None, :] * CMP_STRIDE + CMP_LEN - 1))
    bias_c = bias_c.reshape(g_, r_, t, n_c)
    strip = _bias_at(rel_bias, jnp.arange(tq)[:, None] + m_off - jnp.arange(width)[None, :])
    strip = strip.reshape(g_, r_, tq, width)
    kv_spec = pl.BlockSpec((1, 1, t, HEAD_DIM), lambda bi, g, qi: (bi, g, 0, 0))
    kc_spec = pl.BlockSpec((1, 1, n_c, HEAD_DIM), lambda bi, g, qi: (bi, g, 0, 0))
    ovl = _overlap(n_c, n_slc, LANES)
    ex = _expand(LANES, t)
    ge = _gate_expand()
    kern = functools.partial(_nsa_prompt_kernel, tq=tq, tk=tk, n_c=n_c, n_slc=n_slc, n_top=n_top, m_off=m_off)
    return pl.pallas_call(
        kern, out_shape=SDS((b, t, NSA_W), F32), grid=(b, g_, t // tq),
        in_specs=[pl.BlockSpec((1, 1, r_, tq, HEAD_DIM), lambda bi, g, qi: (bi, g, 0, qi, 0)),
                  kc_spec, kc_spec, kv_spec, kv_spec, kv_spec, kv_spec,
                  pl.BlockSpec((1, tq, LANES), lambda bi, g, qi: (bi, qi, 0)),
                  pl.BlockSpec((1,) + ge.shape[1:], lambda bi, g, qi: (g, 0, 0)),
                  pl.BlockSpec((1, r_, tq, n_c), lambda bi, g, qi: (g, 0, qi, 0)),
                  pl.BlockSpec((1, r_, tq, width), lambda bi, g, qi: (g, 0, 0, 0)),
                  pl.BlockSpec(ovl.shape, lambda bi, g, qi: (0, 0)),
                  pl.BlockSpec(ex.shape, lambda bi, g, qi: (0, 0))],
        out_specs=pl.BlockSpec((1, tq, r_ * HEAD_DIM), lambda bi, g, qi: (bi, qi, g)),
        compiler_params=_cparams(("arbitrary", "arbitrary", "arbitrary")), name="nsa_prompt",
    )(qh, kch[0], kch[1], nsah[2], nsah[3], winh[0], winh[1], gates, ge, bias_c, strip, ovl, ex)


def _tokens_last(cache):
    return cache.transpose(0, 1, 3, 4, 5, 2)


def _page_copies(page_src, pt_ref, seq, first_page, n_pages, buf, slot, sem):
    return [pltpu.make_async_copy(page_src(pt_ref[seq, first_page + i]), buf.at[slot, i], sem.at[slot])
            for i in range(n_pages)]


def _paged_pipeline(page_src, pt_ref, n_seq, n_chunk, chunk_pages, buf, sem, reverse):
    seq, j = pl.program_id(0), pl.program_id(1)
    step = seq * n_chunk + j

    def start(st):
        s_, j_ = st // n_chunk, st % n_chunk
        first = ((n_chunk - 1 - j_) if reverse else j_) * chunk_pages
        for cp in _page_copies(page_src, pt_ref, s_, first, chunk_pages, buf, st % 2, sem):
            cp.start()

    @pl.when(step == 0)
    def _():
        start(step)

    @pl.when(step + 1 < n_seq * n_chunk)
    def _():
        start(step + 1)

    slot = step % 2
    for cp in _page_copies(page_src, pt_ref, 0, 0, chunk_pages, buf, slot, sem):
        cp.wait()
    return slot


def _sb_sample_kernel(pt_ref, qbd_ref, new_ref, tri_ref, cache_ref, o_ref, buf, sem, carry_scr, acc_scr,
                      *, layer, n_seq, n_chunk, chunk_pages, page_rows, ts):
    j = pl.program_id(1)
    slot = _paged_pipeline(lambda page: cache_ref.at[layer, page], pt_ref, n_seq, n_chunk, chunk_pages, buf,
                           sem, reverse=True)
    rows = ts * SB_HEADS
    qbd = (qbd_ref[0] * SCALE).astype(BF16)
    tri = tri_ref[...]
    sub = tri.shape[0]

    def cumsum_tiles(lb, lk, carry, mask):
        atts = []
        for s in reversed(range(lb.shape[1] // sub)):
            sl = slice(s * sub, (s + 1) * sub)
            att = jnp.exp(lb[:, sl] + _dot_hl(lk[:, sl], tri) + carry)
            atts.append(att if mask is None else jnp.where(mask[:, sl], att, 0.0))
            carry = carry + jnp.sum(lk[:, sl], axis=1, keepdims=True)
        return jnp.concatenate(atts[::-1], axis=1), carry

    @pl.when(j == 0)
    def _():
        n_new = new_ref.shape[1]
        lb, lk = _stick_terms(_dot_nt(qbd, new_ref[0, :, 0:SB_W]))
        q_idx = lax.broadcasted_iota(jnp.int32, (rows, n_new), 0) // SB_HEADS
        causal = lax.broadcasted_iota(jnp.int32, (rows, n_new), 1) < q_idx
        lk = jnp.where(causal, lk, 0.0)
        att, carry = cumsum_tiles(lb, lk, jnp.zeros((rows, 1), F32), causal)
        carry_scr[...] = carry
        acc_scr[...] = _dot(att, new_ref[0, :, SB_W:2 * SB_W])

    z = jnp.concatenate([_dot(qbd, buf[slot, i, 0].reshape(SB_W, page_rows)) for i in range(chunk_pages)],
                        axis=1)
    lb, lk = _stick_terms(z)
    att, carry = cumsum_tiles(lb, lk, carry_scr[...], None)
    carry_scr[...] = carry
    acc = acc_scr[...]
    for i in range(chunk_pages):
        acc = acc + _dot_nt(att[:, i * page_rows:(i + 1) * page_rows], buf[slot, i, 1].reshape(SB_W, page_rows))
    acc_scr[...] = acc

    @pl.when(j == n_chunk - 1)
    def _():
        acc = acc_scr[...]
        head_of_row = lax.broadcasted_iota(jnp.int32, acc.shape, 0) % SB_HEADS
        head_of_lane = lax.broadcasted_iota(jnp.int32, acc.shape, 1) // HEAD_DIM
        own = jnp.where(head_of_row == head_of_lane, acc, 0.0)
        o_ref[0] = jnp.sum(own.reshape(ts, SB_HEADS, SB_W), axis=1)


def _sb_sample(layer, q_sb, sb_rows, cache_sb, page_table, chunk_pages, sub):
    bs, ts, _ = q_sb.shape
    n_pages = page_table.shape[1]
    page_rows = cache_sb.shape[-1]
    n_chunk = n_pages // chunk_pages
    rows = ts * SB_HEADS
    eye = jnp.eye(SB_HEADS, dtype=F32)
    qbd = jnp.einsum("bqhd,hk->bqhkd", q_sb.reshape(bs, ts, SB_HEADS, HEAD_DIM), eye).reshape(bs, rows, SB_W)
    new = jnp.pad(sb_rows, ((0, 0), (0, sub - ts), (0, 0)))
    kern = functools.partial(_sb_sample_kernel, layer=layer, n_seq=bs, n_chunk=n_chunk,
                             chunk_pages=chunk_pages, page_rows=page_rows, ts=ts)
    grid_spec = pltpu.PrefetchScalarGridSpec(
        num_scalar_prefetch=1, grid=(bs, n_chunk),
        in_specs=[pl.BlockSpec((1, rows, SB_W), lambda b, j, pt: (b, 0, 0)),
                  pl.BlockSpec((1, sub, 2 * SB_W), lambda b, j, pt: (b, 0, 0)),
                  pl.BlockSpec((sub, sub), lambda b, j, pt: (0, 0)),
                  pl.BlockSpec(memory_space=pl.ANY)],
        out_specs=pl.BlockSpec((1, ts, SB_W), lambda b, j, pt: (b, 0, 0)),
        scratch_shapes=[pltpu.VMEM((2, chunk_pages) + cache_sb.shape[2:], F32),
                        pltpu.SemaphoreType.DMA((2,)),
                        pltpu.VMEM((rows, 1), F32), pltpu.VMEM((rows, SB_W), F32)])
    return pl.pallas_call(
        kern, out_shape=SDS((bs, ts, SB_W), F32), grid_spec=grid_spec,
        compiler_params=_cparams(("arbitrary", "arbitrary")), name="sb_sample",
    )(page_table, qbd, new, _tri(sub), cache_sb)


def _cmp_sample_kernel(pt_ref, w1p_ref, w2p_ref, pe_ref, w1cat_ref, cache_ref, o_ref, buf, sem, slab_scr, u_scr,
                       *, layer, n_seq, n_chunk, chunk_pages, page_rows):
    j = pl.program_id(1)
    slot = _paged_pipeline(lambda page: cache_ref.at[layer, page, pl.ds(0, 2)], pt_ref, n_seq, n_chunk,
                           chunk_pages, buf, sem, reverse=False)

    def to_rows(i, _):
        for kind in range(2):
            slab_scr[kind, pl.ds(pl.multiple_of(i * page_rows, page_rows), page_rows), :] = (
                buf[slot, i, kind].reshape(LANES, page_rows).T)
        return 0

    lax.fori_loop(0, chunk_pages, to_rows, 0)
    n_step = chunk_pages * page_rows // CMP_STRIDE
    for kind in range(2):
        u_scr[kind, pl.ds(pl.multiple_of(j * n_step, n_step), n_step), :] = _compress_partial(
            slab_scr.at[kind], w1p_ref, kind, n_step)

    @pl.when(j == n_chunk - 1)
    def _():
        for kind in range(2):
            o_ref[0, :, kind * LANES:(kind + 1) * LANES] = _compress_finish(
                u_scr[kind], kind, w2p_ref, pe_ref, w1cat_ref)


def _cmp_sample(layer, cache_nsa, page_table, cw, chunk_pages):
    bs, n_pages = page_table.shape
    page_rows = cache_nsa.shape[-1]
    assert page_rows == LANES and KV_W == LANES
    n_chunk = n_pages // chunk_pages
    n_c = n_pages * page_rows // CMP_STRIDE
    w1p, w2p, pe, w1cat = cw
    full = lambda a: pl.BlockSpec(a.shape, lambda b, j, pt: (0,) * a.ndim)
    kern = functools.partial(_cmp_sample_kernel, layer=layer, n_seq=bs, n_chunk=n_chunk,
                             chunk_pages=chunk_pages, page_rows=page_rows)
    grid_spec = pltpu.PrefetchScalarGridSpec(
        num_scalar_prefetch=1, grid=(bs, n_chunk),
        in_specs=[full(w1p), full(w2p), full(pe), full(w1cat), pl.BlockSpec(memory_space=pl.ANY)],
        out_specs=pl.BlockSpec((1, n_c, 2 * LANES), lambda b, j, pt: (b, 0, 0)),
        scratch_shapes=[pltpu.VMEM((2, chunk_pages, 2) + cache_nsa.shape[3:], F32),
                        pltpu.SemaphoreType.DMA((2,)),
                        pltpu.VMEM((2, chunk_pages * page_rows, LANES), F32),
                        pltpu.VMEM((2, n_c, 2 * LANES), F32)])
    return pl.pallas_call(
        kern, out_shape=SDS((bs, n_c, 2 * LANES), F32), grid_spec=grid_spec,
        compiler_params=_cparams(("arbitrary", "arbitrary")), name="cmp_sample",
    )(page_table, w1p, w2p, pe, w1cat, cache_nsa)


def _rank_rows(score, n_top):
    rows, nb = score.shape
    ii = lax.broadcasted_iota(jnp.int32, (nb, nb), 0)
    jj = lax.broadcasted_iota(jnp.int32, (nb, nb), 1)
    out = []
    for r in range(rows):
        s_row = score[r:r + 1, :]
        s_col = jnp.sum(jnp.where(ii == jj, s_row, 0.0), axis=1, keepdims=True)
        beats = (s_col > s_row) | ((s_col == s_row) & (ii < jj))
        rank = jnp.sum(jnp.where(beats, 1.0, 0.0), axis=0, keepdims=True)
        out.append(jnp.where(rank < n_top, 1.0, 0.0))
    return jnp.concatenate(out, axis=0)


def _nsa_a_kernel(qbd_ref, kcvc_ref, win_ref, wnew_ref, bc_ref, bwp_ref, bwn_ref, ovl_ref,
                  oc_ref, ow_ref, sel_ref, *, ts, past, n_slc, n_top):
    rows = NSA_REP * NSA_KV * ts
    gq = NSA_KV * ts
    qbd = (qbd_ref[0] * SCALE).astype(BF16)
    q_idx = lax.broadcasted_iota(jnp.int32, (rows, 1), 0) % ts
    t_pos = past + q_idx

    n_c = kcvc_ref.shape[1]
    c_idx = lax.broadcasted_iota(jnp.int32, (1, n_c), 1)
    mc = ((CMP_STRIDE * c_idx + CMP_LEN - 1) <= t_pos) & (c_idx < n_c - 1)
    pc = _masked_softmax(_dot_nt(qbd, kcvc_ref[0, :, 0:KV_W]) + bc_ref[...], mc)
    oc_ref[0] = _dot(pc, kcvc_ref[0, :, KV_W:2 * KV_W])

    pcs = pc[0:gq]
    for r in range(1, NSA_REP):
        pcs = pcs + pc[r * gq:(r + 1) * gq]
    imp = _dot_hl(pcs, ovl_ref[...])
    nb = imp.shape[1]
    blk = lax.broadcasted_iota(jnp.int32, (gq, nb), 1)
    cur = (past + lax.broadcasted_iota(jnp.int32, (gq, 1), 0) % ts) >> 6
    valid = blk <= cur
    forced = valid & ((blk == 0) | (blk >= cur - 1))
    score = jnp.where(forced, FORCE, jnp.where(valid, imp, -FORCE))
    sel_ref[0] = jnp.where(valid, _rank_rows(score, n_top), 0.0)

    w_buf = win_ref.shape[-1]
    r_idx = lax.broadcasted_iota(jnp.int32, (1, w_buf), 1)
    d_past = w_buf + q_idx - r_idx
    l1 = _dot(qbd, win_ref[0, 0, 0].reshape(KV_W, w_buf)) + bwp_ref[...]
    m1 = (d_past >= 0) & (d_past < WINDOW)
    n_new = wnew_ref.shape[1]
    d_new = q_idx - lax.broadcasted_iota(jnp.int32, (1, n_new), 1)
    l2 = _dot_nt(qbd, wnew_ref[0, :, 0:KV_W]) + bwn_ref[...]
    m2 = (d_new >= 0) & (d_new < WINDOW)
    l1 = jnp.where(m1, l1, NEG_INF)
    l2 = jnp.where(m2, l2, NEG_INF)
    m = jnp.maximum(jnp.max(l1, axis=1, keepdims=True), jnp.max(l2, axis=1, keepdims=True))
    e1 = jnp.where(m1, jnp.exp(l1 - m), 0.0)
    e2 = jnp.where(m2, jnp.exp(l2 - m), 0.0)
    den = jnp.maximum(jnp.sum(e1, axis=1, keepdims=True) + jnp.sum(e2, axis=1, keepdims=True), TINY)
    ow_ref[0] = (_dot_nt(e1 / den, win_ref[0, 0, 1].reshape(KV_W, w_buf))
                 + _dot(e2 / den, wnew_ref[0, :, KV_W:2 * KV_W]))


def _nsa_b_kernel(pt_ref, qbd_ref, snew_ref, sel_ref, selnew_ref, bs_ref, bsn_ref, ex_ref, oc_ref, ow_ref,
                  gate_ref, cache_ref, o_ref, buf, sem, m_scr, l_scr, acc_scr,
                  *, layer, n_seq, n_chunk, chunk_pages, page_rows, ts):
    j = pl.program_id(1)
    slot = _paged_pipeline(lambda page: cache_ref.at[layer, page, pl.ds(2, 2)], pt_ref, n_seq, n_chunk,
                           chunk_pages, buf, sem, reverse=False)
    rows = NSA_REP * NSA_KV * ts
    qbd = (qbd_ref[0] * SCALE).astype(BF16)
    q_idx = lax.broadcasted_iota(jnp.int32, (rows, 1), 0) % ts

    def online(lg, mk, weigh):
        lg = jnp.where(mk, lg, NEG_INF)
        m_old = m_scr[...]
        m_new = jnp.maximum(m_old, jnp.max(lg, axis=1, keepdims=True))
        alpha = jnp.exp(m_old - m_new)
        p = jnp.where(mk, jnp.exp(lg - m_new), 0.0)
        m_scr[...] = m_new
        l_scr[...] = alpha * l_scr[...] + jnp.sum(p, axis=1, keepdims=True)
        acc_scr[...] = alpha * acc_scr[...] + weigh(p)

    def rows_of(x):
        return jnp.concatenate([x] * NSA_REP, axis=0)

    @pl.when(j == 0)
    def _():
        m_scr[...] = jnp.full_like(m_scr, NEG_INF)
        l_scr[...] = jnp.zeros_like(l_scr)
        acc_scr[...] = jnp.zeros_like(acc_scr)
        n_new = snew_ref.shape[1]
        d_new = q_idx - lax.broadcasted_iota(jnp.int32, (1, n_new), 1)
        chosen = rows_of(selnew_ref[0])[:, 0:1] > 0.5
        online(_dot_nt(qbd, snew_ref[0, :, 0:KV_W]) + bsn_ref[...], chosen & (d_new >= 0),
               lambda p: _dot(p, snew_ref[0, :, KV_W:2 * KV_W]))

    selk = rows_of(jnp.dot(sel_ref[0, 0].astype(BF16), ex_ref[...], preferred_element_type=F32))
    lg = jnp.concatenate([_dot(qbd, buf[slot, i, 0].reshape(KV_W, page_rows)) for i in range(chunk_pages)],
                         axis=1)

    def weigh(p):
        pv = _dot_nt(p[:, 0:page_rows], buf[slot, 0, 1].reshape(KV_W, page_rows))
        for i in range(1, chunk_pages):
            pv = pv + _dot_nt(p[:, i * page_rows:(i + 1) * page_rows], buf[slot, i, 1].reshape(KV_W, page_rows))
        return pv

    online(lg + bs_ref[0], selk > 0.5, weigh)

    @pl.when(j == n_chunk - 1)
    def _():
        o_s = acc_scr[...] / jnp.maximum(l_scr[...], TINY)
        g = _sigmoid(gate_ref[0])
        o_ref[0] = g[:, 0:1] * oc_ref[0] + g[:, 1:2] * o_s + g[:, 2:3] * ow_ref[0]


def _nsa_sample(layer, q_nsa, nsa_rows, win_rows, gates, kcvc, state_win, cache_nsa, page_table, rel_bias,
                chunk_pages):
    bs, ts, _ = q_nsa.shape
    n_pages = page_table.shape[1]
    page_rows = cache_nsa.shape[-1]
    past = n_pages * page_rows
    n_c = kcvc.shape[1]
    w_buf = state_win.shape[-1]
    n_slc = -(-(past + ts) // SLC_BLOCK)
    n_top = min(SLC_TOP, n_slc)
    n_past_blk = past // SLC_BLOCK
    nb_pad = -(-n_slc // LANES) * LANES
    g_, r_ = NSA_KV, NSA_REP
    rows = r_ * g_ * ts
    new_pad = LANES
    assert past % SLC_BLOCK == 0 and ts <= SLC_BLOCK and w_buf == WINDOW

    q5 = q_nsa.reshape(bs, ts, g_, r_, HEAD_DIM)
    qbd = jnp.einsum("bqgrd,gk->brgqkd", q5, jnp.eye(g_, dtype=F32)).reshape(bs, rows, KV_W)
    qq = jnp.arange(ts)

    def row_bias(dist_qk):
        bias = _bias_at(rel_bias, dist_qk).reshape(g_, r_, ts, -1)
        return bias.transpose(1, 0, 2, 3).reshape(rows, -1)

    bias_c = row_bias(past + qq[:, None] - (jnp.arange(n_c)[None, :] * CMP_STRIDE + CMP_LEN - 1))
    bias_wp = row_bias(w_buf + qq[:, None] - jnp.arange(w_buf)[None, :])
    bias_new = row_bias(qq[:, None] - jnp.arange(new_pad)[None, :])
    chunk_keys = chunk_pages * page_rows
    n_chunk = n_pages // chunk_pages
    bias_s = row_bias(past + qq[:, None] - jnp.arange(past)[None, :])
    bias_s = bias_s.reshape(rows, n_chunk, chunk_keys).transpose(1, 0, 2)
    ovl = _overlap(n_c, n_slc, nb_pad)
    wnew = jnp.pad(win_rows, ((0, 0), (0, new_pad - ts), (0, 0)))
    snew = jnp.pad(nsa_rows[:, :, 2 * KV_W:4 * KV_W], ((0, 0), (0, new_pad - ts), (0, 0)))

    full = lambda a: pl.BlockSpec(a.shape, lambda b: (0,) * a.ndim)
    oc, ow, sel = pl.pallas_call(
        functools.partial(_nsa_a_kernel, ts=ts, past=past, n_slc=n_slc, n_top=n_top),
        out_shape=[SDS((bs, rows, KV_W), F32), SDS((bs, rows, KV_W), F32), SDS((bs, g_ * ts, nb_pad), F32)],
        grid=(bs,),
        in_specs=[pl.BlockSpec((1, rows, KV_W), lambda b: (b, 0, 0)),
                  pl.BlockSpec((1, n_c, 2 * KV_W), lambda b: (b, 0, 0)),
                  pl.BlockSpec((1, 1) + state_win.shape[2:], lambda b: (layer, b, 0, 0, 0, 0)),
                  pl.BlockSpec((1, new_pad, 2 * KV_W), lambda b: (b, 0, 0)),
                  full(bias_c), full(bias_wp), full(bias_new), full(ovl)],
        out_specs=[pl.BlockSpec((1, rows, KV_W), lambda b: (b, 0, 0)),
                   pl.BlockSpec((1, rows, KV_W), lambda b: (b, 0, 0)),
                   pl.BlockSpec((1, g_ * ts, nb_pad), lambda b: (b, 0, 0))],
        compiler_params=_cparams(("arbitrary",)), name="nsa_sample_a",
    )(qbd, kcvc, state_win, wnew, bias_c, bias_wp, bias_new, ovl)

    blk_per_chunk = chunk_keys // SLC_BLOCK
    sel_chunks = sel[:, :, :n_past_blk].reshape(bs, g_ * ts, n_chunk, blk_per_chunk).transpose(0, 2, 1, 3)
    sel_chunks = jnp.pad(sel_chunks, ((0, 0), (0, 0), (0, 0), (0, LANES - blk_per_chunk)))
    sel_new = jnp.pad(sel[:, :, n_past_blk:n_past_blk + 1], ((0, 0), (0, 0), (0, LANES - 1)))
    ex = _expand(LANES, chunk_keys)
    gt = gates[:, :, :GATE_W].reshape(bs, ts, g_, r_, 3).transpose(0, 3, 2, 1, 4).reshape(bs, rows, 3)
    gt = jnp.pad(gt, ((0, 0), (0, 0), (0, LANES - 3)))

    kern = functools.partial(_nsa_b_kernel, layer=layer, n_seq=bs, n_chunk=n_chunk, chunk_pages=chunk_pages,
                             page_rows=page_rows, ts=ts)
    per_seq = lambda a: pl.BlockSpec((1,) + a.shape[1:], lambda b, j, pt: (b,) + (0,) * (a.ndim - 1))
    const = lambda a: pl.BlockSpec(a.shape, lambda b, j, pt: (0,) * a.ndim)
    grid_spec = pltpu.PrefetchScalarGridSpec(
        num_scalar_prefetch=1, grid=(bs, n_chunk),
        in_specs=[per_seq(qbd), per_seq(snew),
                  pl.BlockSpec((1, 1, g_ * ts, LANES), lambda b, j, pt: (b, j, 0, 0)),
                  per_seq(sel_new),
                  pl.BlockSpec((1, rows, chunk_keys), lambda b, j, pt: (j, 0, 0)),
                  const(bias_new), const(ex), per_seq(oc), per_seq(ow), per_seq(gt),
                  pl.BlockSpec(memory_space=pl.ANY)],
        out_specs=pl.BlockSpec((1, rows, KV_W), lambda b, j, pt: (b, 0, 0)),
        scratch_shapes=[pltpu.VMEM((2, chunk_pages, 2) + cache_nsa.shape[3:], F32),
                        pltpu.SemaphoreType.DMA((2,)),
                        pltpu.VMEM((rows, 1), F32), pltpu.VMEM((rows, 1), F32), pltpu.VMEM((rows, KV_W), F32)])
    o = pl.pallas_call(
        kern, out_shape=SDS((bs, rows, KV_W), F32), grid_spec=grid_spec,
        compiler_params=_cparams(("arbitrary", "arbitrary")), name="nsa_sample_b",
    )(page_table, qbd, snew, sel_chunks, sel_new, bias_s, bias_new, ex, oc, ow, gt, cache_nsa)
    o = o.reshape(bs, r_, g_, ts, g_, HEAD_DIM)
    o = jnp.stack([o[:, :, g, :, g, :] for g in range(g_)], axis=2)
    return o.transpose(0, 3, 2, 1, 4).reshape(bs, ts, NSA_W)


def _layer_weights(l, w_in, w_out, w_up, w_down, cmp_pe, cmp_w1, cmp_w2):
    w_in_bf = jnp.pad(w_in[l], ((0, 0), (0, D_IN_PAD - D_IN))).astype(BF16)
    return dict(w_in=w_in_bf, w_out=w_out[l].astype(BF16), w_up=w_up[l].astype(BF16),
                w_down=w_down[l].astype(BF16), cmp=_compress_weights(cmp_pe[l], cmp_w1[l], cmp_w2[l]))


def _dense_tail(o_sb, o_nsa, x2d, mod, lw, g_post_mix, g_pre_ffn, g_post_ffn, tm, tf):
    x1 = _out_proj(o_sb, o_nsa, x2d, g_post_mix, mod, lw["w_out"], tm)
    return _ffn(x1, g_pre_ffn, g_post_ffn, mod, lw["w_up"], lw["w_down"], tm, tf)


def kernel(x_prompt, x_sample, c_prompt, c_sample, cache_sb_kv, cache_nsa_kv, state_nsa_win, page_table, w_ada, b_ada, g_pre_mix, g_post_mix, g_pre_ffn, g_post_ffn, w_in, w_out, cmp_pe, cmp_w1, cmp_w2, rel_bias, w_up, w_down):
    bp, t, d = x_prompt.shape
    bs, ts, _ = x_sample.shape
    depth = w_in.shape[0]
    n_phys, page_rows = cache_sb_kv.shape[1], cache_sb_kv.shape[2]
    n_pages = page_table.shape[1]
    w_buf = state_nsa_win.shape[2]
    f = w_up.shape[2]
    page_table = page_table.astype(jnp.int32)

    cache_sb = _tokens_last(cache_sb_kv)
    cache_nsa = _tokens_last(cache_nsa_kv)
    state_win = _tokens_last(state_nsa_win)

    tm_p = 256
    tm_ffn_p = 512 if t % 512 == 0 else tm_p
    assert t % tm_p == 0
    tf = 1024 if f % 1024 == 0 else f
    tq_sb = 256
    sb_chunk_pages = math.gcd(n_pages, 16)
    cmp_chunk_pages = math.gcd(n_pages, 64)
    slc_chunk_pages = math.gcd(n_pages, 16)

    mod = _ada(jnp.concatenate([c_prompt, c_sample], axis=0), w_ada, b_ada)

    y_p = x_prompt.reshape(bp * t, d)
    y_s = x_sample.reshape(bs * ts, d)
    outs = {k: [] for k in ("sb_p", "sb_s", "nsa_p", "nsa_s", "win_p", "win_s")}
    for l in range(depth):
        lw = _layer_weights(l, w_in, w_out, w_up, w_down, cmp_pe, cmp_w1, cmp_w2)

        mod_p = _Mod(mod[l, :bp].reshape(bp, 1, 6 * d), t)
        q_sb, sb_rows, q_nsa, nsa_rows, win_rows, gates = _in_proj(y_p, g_pre_mix[l], mod_p, lw["w_in"], tm_p)
        o_sb = _sb_prompt(q_sb.reshape(bp, t, SB_W), sb_rows.reshape(bp, t, 2 * SB_W), tq_sb)
        nsa3 = nsa_rows.reshape(bp, t, 4 * KV_W)
        win3 = win_rows.reshape(bp, t, 2 * KV_W)
        kcvc = _cmp_prompt(nsa3, lw["cmp"])
        o_nsa = _nsa_prompt(q_nsa.reshape(bp, t, NSA_W), nsa3, win3, gates.reshape(bp, t, LANES), kcvc,
                            rel_bias, 128, 512 if t % 512 == 0 else 256)
        y_p = _dense_tail(o_sb.reshape(bp * t, SB_W), o_nsa.reshape(bp * t, NSA_W), y_p, mod_p, lw,
                          g_post_mix[l], g_pre_ffn[l], g_post_ffn[l], tm_ffn_p, tf)
        outs["sb_p"].append(sb_rows.reshape(bp, t, 2, SB_HEADS, HEAD_DIM))
        outs["nsa_p"].append(nsa_rows.reshape(bp, t, 4, NSA_KV, HEAD_DIM))
        wp = win3 if t >= w_buf else jnp.pad(win3, ((0, 0), (w_buf - t, 0), (0, 0)))
        outs["win_p"].append(wp[:, wp.shape[1] - w_buf:].reshape(bp, w_buf, 2, NSA_KV, HEAD_DIM))

        tm_s = bs * ts
        mod_s = _Mod(jnp.repeat(mod[l, bp:], ts, axis=0).reshape(1, bs * ts, 6 * d), 1)
        q_sb, sb_rows, q_nsa, nsa_rows, win_rows, gates = _in_proj(y_s, g_pre_mix[l], mod_s, lw["w_in"], tm_s)
        sb3 = sb_rows.reshape(bs, ts, 2 * SB_W)
        nsa3 = nsa_rows.reshape(bs, ts, 4 * KV_W)
        win3 = win_rows.reshape(bs, ts, 2 * KV_W)
        o_sb = _sb_sample(l, q_sb.reshape(bs, ts, SB_W), sb3, cache_sb, page_table, sb_chunk_pages, 256)
        kcvc = _cmp_sample(l, cache_nsa, page_table, lw["cmp"], cmp_chunk_pages)
        o_nsa = _nsa_sample(l, q_nsa.reshape(bs, ts, NSA_W), nsa3, win3, gates.reshape(bs, ts, LANES), kcvc,
                            state_win, cache_nsa, page_table, rel_bias, slc_chunk_pages)
        y_s = _dense_tail(o_sb.reshape(bs * ts, SB_W), o_nsa.reshape(bs * ts, NSA_W), y_s, mod_s, lw,
                          g_post_mix[l], g_pre_ffn[l], g_post_ffn[l], tm_s, tf)
        outs["sb_s"].append(sb_rows.reshape(bs, ts, 2, SB_HEADS, HEAD_DIM))
        outs["nsa_s"].append(nsa_rows.reshape(bs, ts, 4, NSA_KV, HEAD_DIM))
        win_all = jnp.concatenate([state_nsa_win[l], win3.reshape(bs, ts, 2, NSA_KV, HEAD_DIM)], axis=1)
        outs["win_s"].append(win_all[:, win_all.shape[1] - w_buf:])

    return (y_p.reshape(bp, t, d), y_s.reshape(bs, ts, d), jnp.stack(outs["sb_p"]), jnp.stack(outs["sb_s"]),
            jnp.stack(outs["nsa_p"]), jnp.stack(outs["nsa_s"]), jnp.stack(outs["win_p"]),
            jnp.stack(outs["win_s"]))
```

```python
import functools
import math

import jax
import jax.numpy as jnp
import numpy as np
from jax import lax
from jax.experimental import pallas as pl
from jax.experimental.pallas import tpu as pltpu

F32 = jnp.float32
BF16 = jnp.bfloat16
SDS = jax.ShapeDtypeStruct

HEAD_DIM = 64
SB_HEADS = 8
NSA_KV = 2
NSA_REP = 4
NSA_HEADS = NSA_KV * NSA_REP
SB_W = SB_HEADS * HEAD_DIM
NSA_W = NSA_HEADS * HEAD_DIM
KV_W = NSA_KV * HEAD_DIM
GATE_W = 3 * NSA_HEADS
D_IN = 3 * SB_W + NSA_W + 6 * KV_W + GATE_W
CMP_STRIDE = 16
CMP_LEN = 2 * CMP_STRIDE
SLC_BLOCK = 64
SLC_TOP = 16
WINDOW = 512
N_BUCKETS = 32
MAX_DISTANCE = 1024
RMS_EPS = 1e-6
NEG_INF = -1e30
FORCE = 1e9
TINY = 1e-30
SCALE = HEAD_DIM ** -0.5

LANES = 128
VMEM_LIMIT = 52 * 1024 * 1024


def _cparams(sem):
    return pltpu.CompilerParams(dimension_semantics=sem, vmem_limit_bytes=VMEM_LIMIT)


def _dot(a, b):
    return jnp.dot(a.astype(BF16), b.astype(BF16), preferred_element_type=F32)


def _dot_nt(a, b):
    return lax.dot_general(a.astype(BF16), b.astype(BF16), (((1,), (1,)), ((), ())),
                           preferred_element_type=F32)


def _split(a):
    hi = a.astype(BF16)
    lo = (a - hi.astype(F32)).astype(BF16)
    return hi, lo


def _dot_hl(a, b_bf16):
    hi, lo = _split(a)
    return (jnp.dot(hi, b_bf16, preferred_element_type=F32)
            + jnp.dot(lo, b_bf16, preferred_element_type=F32))


def _dot3(a, b):
    a_hi, a_lo = _split(a)
    b_hi, b_lo = _split(b)
    return (jnp.dot(a_hi, b_hi, preferred_element_type=F32)
            + jnp.dot(a_hi, b_lo, preferred_element_type=F32)
            + jnp.dot(a_lo, b_hi, preferred_element_type=F32))


def _sigmoid(x):
    return 1.0 / (1.0 + jnp.exp(-x))


def _rms(x, g):
    return x * lax.rsqrt(jnp.mean(x * x, axis=-1, keepdims=True) + RMS_EPS) * g


def _masked_softmax(lg, mask):
    lg = jnp.where(mask, lg, NEG_INF)
    m = jnp.max(lg, axis=-1, keepdims=True)
    e = jnp.where(mask, jnp.exp(lg - m), 0.0)
    return e / jnp.maximum(jnp.sum(e, axis=-1, keepdims=True), TINY)


def _stick_terms(z):
    sp = jnp.log(1.0 + jnp.exp(-jnp.abs(z)))
    lb = jnp.minimum(z, 0.0) - sp
    return lb, lb - z


def _gelu_tanh(x):
    return 0.5 * x * (1.0 + jnp.tanh(math.sqrt(2.0 / math.pi) * (x + 0.044715 * (x * x * x))))


def _ada_kernel(c_ref, w_ref, b_ref, o_ref):
    c = c_ref[...]
    o_ref[0] = _dot3(c * _sigmoid(c), w_ref[0]) + b_ref[0]


def _ada(c_all, w_ada, b_ada):
    depth, d, d6 = w_ada.shape
    nb = c_all.shape[0]
    tn = d6 // 4
    return pl.pallas_call(
        _ada_kernel, out_shape=SDS((depth, nb, d6), F32), grid=(depth, d6 // tn),
        in_specs=[pl.BlockSpec((nb, d), lambda l, j: (0, 0)),
                  pl.BlockSpec((1, d, tn), lambda l, j: (l, 0, j)),
                  pl.BlockSpec((1, 1, tn), lambda l, j: (l, 0, j))],
        out_specs=pl.BlockSpec((1, nb, tn), lambda l, j: (l, 0, j)),
        compiler_params=_cparams(("arbitrary", "arbitrary")), name="ada",
    )(c_all, w_ada, b_ada.reshape(depth, 1, d6))


def _mod_spec(mod3, d, tm, chunk):
    nb, rb, _ = mod3.shape
    if rb == 1:
        rows_per_b = mod3.rows_per_b
        return pl.BlockSpec((1, 1, d), lambda i, *_: ((i * tm) // rows_per_b, 0, chunk))
    return pl.BlockSpec((1, tm, d), lambda i, *_: (0, i, chunk))


class _Mod:
    def __init__(self, arr, rows_per_b):
        self.arr = arr
        self.shape = arr.shape
        self.rows_per_b = rows_per_b


_PROJ_SPLITS = (("q_sb", 0, SB_W), ("sb_rows", SB_W, 3 * SB_W), ("q_nsa", 3 * SB_W, 3 * SB_W + NSA_W),
                ("nsa_rows", 3 * SB_W + NSA_W, 3 * SB_W + NSA_W + 4 * KV_W),
                ("win_rows", 3 * SB_W + NSA_W + 4 * KV_W, 3 * SB_W + NSA_W + 6 * KV_W),
                ("gates", 3 * SB_W + NSA_W + 6 * KV_W, 3 * SB_W + NSA_W + 6 * KV_W + LANES))
D_IN_PAD = _PROJ_SPLITS[-1][2]


def _in_proj_kernel(x_ref, g_ref, sc_ref, sh_ref, w_ref, *out_refs):
    h = _rms(x_ref[...], g_ref[...]) * (1.0 + sc_ref[0]) + sh_ref[0]
    p = jnp.dot(h.astype(BF16), w_ref[...], preferred_element_type=F32)
    for o_ref, (_, lo, hi) in zip(out_refs, _PROJ_SPLITS):
        o_ref[...] = p[:, lo:hi]


def _in_proj(x2d, g, mod, w_bf, tm):
    r, d = x2d.shape
    widths = [hi - lo for _, lo, hi in _PROJ_SPLITS]
    return pl.pallas_call(
        _in_proj_kernel, out_shape=[SDS((r, w), F32) for w in widths], grid=(r // tm,),
        in_specs=[pl.BlockSpec((tm, d), lambda i: (i, 0)),
                  pl.BlockSpec((1, d), lambda i: (0, 0)),
                  _mod_spec(mod, d, tm, 1), _mod_spec(mod, d, tm, 0),
                  pl.BlockSpec((d, D_IN_PAD), lambda i: (0, 0))],
        out_specs=[pl.BlockSpec((tm, w), lambda i: (i, 0)) for w in widths],
        compiler_params=_cparams(("arbitrary",)), name="in_proj",
    )(x2d, g.reshape(1, d), mod.arr, mod.arr, w_bf)


def _out_proj_kernel(osb_ref, onsa_ref, x_ref, g_ref, gt_ref, w_ref, o_ref):
    mix = (jnp.dot(osb_ref[...].astype(BF16), w_ref[0:SB_W, :], preferred_element_type=F32)
           + jnp.dot(onsa_ref[...].astype(BF16), w_ref[SB_W:SB_W + NSA_W, :], preferred_element_type=F32))
    o_ref[...] = x_ref[...] + gt_ref[0] * _rms(mix, g_ref[...])


def _out_proj(o_sb, o_nsa, x2d, g, mod, w_bf, tm):
    r, d = x2d.shape
    return pl.pallas_call(
        _out_proj_kernel, out_shape=SDS((r, d), F32), grid=(r // tm,),
        in_specs=[pl.BlockSpec((tm, SB_W), lambda i: (i, 0)),
                  pl.BlockSpec((tm, NSA_W), lambda i: (i, 0)),
                  pl.BlockSpec((tm, d), lambda i: (i, 0)),
                  pl.BlockSpec((1, d), lambda i: (0, 0)),
                  _mod_spec(mod, d, tm, 2),
                  pl.BlockSpec((SB_W + NSA_W, d), lambda i: (0, 0))],
        out_specs=pl.BlockSpec((tm, d), lambda i: (i, 0)),
        compiler_params=_cparams(("arbitrary",)), name="out_proj",
    )(o_sb, o_nsa, x2d, g.reshape(1, d), mod.arr, w_bf)


def _ffn_kernel(x_ref, gpre_ref, sc_ref, sh_ref, wup_ref, wdn_ref, gpost_ref, gt_ref, o_ref, h_scr, acc_scr):
    k = pl.program_id(1)

    @pl.when(k == 0)
    def _():
        h = _rms(x_ref[...], gpre_ref[...]) * (1.0 + sc_ref[0]) + sh_ref[0]
        h_scr[...] = h.astype(BF16)
        acc_scr[...] = jnp.zeros_like(acc_scr)

    u = jnp.dot(h_scr[...], wup_ref[...], preferred_element_type=F32)
    a = jnp.square(jnp.maximum(u, 0.0))
    acc_scr[...] += jnp.dot(a.astype(BF16), wdn_ref[...], preferred_element_type=F32)

    @pl.when(k == pl.num_programs(1) - 1)
    def _():
        o_ref[...] = x_ref[...] + gt_ref[0] * _rms(acc_scr[...], gpost_ref[...])


def _ffn(x2d, g_pre, g_post, mod, wup_bf, wdn_bf, tm, tf):
    r, d = x2d.shape
    f = wup_bf.shape[1]
    return pl.pallas_call(
        _ffn_kernel, out_shape=SDS((r, d), F32), grid=(r // tm, f // tf),
        in_specs=[pl.BlockSpec((tm, d), lambda i, k: (i, 0)),
                  pl.BlockSpec((1, d), lambda i, k: (0, 0)),
                  _mod_spec(mod, d, tm, 4), _mod_spec(mod, d, tm, 3),
                  pl.BlockSpec((d, tf), lambda i, k: (0, k)),
                  pl.BlockSpec((tf, d), lambda i, k: (k, 0)),
                  pl.BlockSpec((1, d), lambda i, k: (0, 0)),
                  _mod_spec(mod, d, tm, 5)],
        out_specs=pl.BlockSpec((tm, d), lambda i, k: (i, 0)),
        scratch_shapes=[pltpu.VMEM((tm, d), BF16), pltpu.VMEM((tm, d), F32)],
        compiler_params=_cparams(("arbitrary", "arbitrary")), name="ffn",
    )(x2d, g_pre.reshape(1, d), mod.arr, mod.arr, wup_bf, wdn_bf, g_post.reshape(1, d), mod.arr)


def _tri(n):
    j = np.arange(n)
    return jnp.asarray((j[:, None] > j[None, :]).astype(np.float32), dtype=BF16)


def _sb_prompt_kernel(q_ref, k_ref, v_ref, tri_ref, o_ref, *, tq):
    qi = pl.program_id(2)
    q2 = q_ref[0] * SCALE
    lane = lax.broadcasted_iota(jnp.int32, (tq, LANES), 1)
    qm = jnp.concatenate([jnp.where(lane < HEAD_DIM, q2, 0.0), jnp.where(lane >= HEAD_DIM, q2, 0.0)],
                         axis=0).astype(BF16)
    row = lax.broadcasted_iota(jnp.int32, (2 * tq, tq), 0)
    col = lax.broadcasted_iota(jnp.int32, (2 * tq, tq), 1)
    causal = col < jnp.where(row >= tq, row - tq, row)
    tri = tri_ref[...]

    def tile(c, carry, acc, diagonal):
        k_c = k_ref[0, pl.ds(pl.multiple_of(c * tq, tq), tq), :]
        v_c = v_ref[0, pl.ds(pl.multiple_of(c * tq, tq), tq), :]
        lb, lk = _stick_terms(_dot_nt(qm, k_c))
        if diagonal:
            lk = jnp.where(causal, lk, 0.0)
        tot = _dot_hl(lk, tri) + carry
        att = jnp.exp(lb + tot)
        if diagonal:
            att = jnp.where(causal, att, 0.0)
        acc = acc + _dot(att, v_c)
        carry = carry + jnp.sum(lk, axis=1, keepdims=True)
        return carry, acc

    carry, acc = tile(qi, jnp.zeros((2 * tq, 1), F32), jnp.zeros((2 * tq, LANES), F32), True)
    carry, acc = lax.fori_loop(
        0, qi, lambda i, ca: tile(qi - 1 - i, ca[0], ca[1], False), (carry, acc))
    o_ref[0] = jnp.where(lane < HEAD_DIM, acc[0:tq], acc[tq:2 * tq])


def _sb_prompt(q_sb, sb_rows, tq):
    b, t, _ = q_sb.shape
    npair = SB_W // LANES
    return pl.pallas_call(
        functools.partial(_sb_prompt_kernel, tq=tq), out_shape=SDS((b, t, SB_W), F32),
        grid=(b, npair, t // tq),
        in_specs=[pl.BlockSpec((1, tq, LANES), lambda bi, hp, qi: (bi, qi, hp)),
                  pl.BlockSpec((1, t, LANES), lambda bi, hp, qi: (bi, 0, hp)),
                  pl.BlockSpec((1, t, LANES), lambda bi, hp, qi: (bi, 0, npair + hp)),
                  pl.BlockSpec((tq, tq), lambda bi, hp, qi: (0, 0))],
        out_specs=pl.BlockSpec((1, tq, LANES), lambda bi, hp, qi: (bi, qi, hp)),
        compiler_params=_cparams(("arbitrary", "arbitrary", "arbitrary")), name="sb_prompt",
    )(q_sb, sb_rows, sb_rows, _tri(tq))


def _compress_weights(cmp_pe, cmp_w1, cmp_w2):
    w1r = cmp_w1.reshape(2, CMP_LEN, HEAD_DIM, HEAD_DIM)
    lo, hi = w1r[:, :CMP_STRIDE], w1r[:, CMP_STRIDE:]
    z = jnp.zeros_like(lo)
    top = jnp.concatenate([lo, z, hi, z], axis=-1)
    bot = jnp.concatenate([z, lo, z, hi], axis=-1)
    w1p = jnp.concatenate([top, bot], axis=2).reshape(2, CMP_STRIDE // 2, 4 * HEAD_DIM, 4 * HEAD_DIM).astype(BF16)
    z2 = jnp.zeros_like(cmp_w2)
    w2p = jnp.concatenate([jnp.concatenate([cmp_w2, z2], axis=-1),
                           jnp.concatenate([z2, cmp_w2], axis=-1)], axis=1).astype(BF16)
    pe = jnp.broadcast_to(cmp_pe.reshape(2, 1, CMP_LEN * HEAD_DIM), (2, 8, CMP_LEN * HEAD_DIM))
    w1cat = jnp.concatenate([cmp_w1, cmp_w1], axis=-1)
    return w1p, w2p, pe, w1cat


def _compress_partial(slab_ref, w1p_ref, kind, n):
    u = jnp.zeros((n, 2 * LANES), F32)
    for p in range(0, CMP_STRIDE, 2):
        x = jnp.concatenate([slab_ref[pl.ds(p, n, stride=CMP_STRIDE), :].astype(BF16),
                             slab_ref[pl.ds(p + 1, n, stride=CMP_STRIDE), :].astype(BF16)], axis=1)
        u = u + jnp.dot(x, w1p_ref[kind, p // 2], preferred_element_type=F32)
    return u


def _compress_finish(u, kind, w2p_ref, pe_ref, w1cat_ref):
    n = u.shape[0]
    pe_term = _dot3(pe_ref[kind], w1cat_ref[kind])[0:1]
    nxt = pltpu.roll(u[:, LANES:], n - 1, 0)
    h = _gelu_tanh(u[:, :LANES] + nxt + pe_term)
    return jnp.dot(h.astype(BF16), w2p_ref[kind], preferred_element_type=F32)


def _cmp_prompt_kernel(k_ref, v_ref, w1p_ref, w2p_ref, pe_ref, w1cat_ref, o_ref, *, n):
    for kind, ref in enumerate((k_ref, v_ref)):
        u = _compress_partial(ref.at[0], w1p_ref, kind, n)
        o_ref[0, :, kind * LANES:(kind + 1) * LANES] = _compress_finish(u, kind, w2p_ref, pe_ref, w1cat_ref)


def _cmp_prompt(nsa_rows, cw):
    b, t, _ = nsa_rows.shape
    n = t // CMP_STRIDE
    w1p, w2p, pe, w1cat = cw
    full = lambda a: pl.BlockSpec(a.shape, lambda bi: (0,) * a.ndim)
    return pl.pallas_call(
        functools.partial(_cmp_prompt_kernel, n=n), out_shape=SDS((b, n, 2 * LANES), F32), grid=(b,),
        in_specs=[pl.BlockSpec((1, t, LANES), lambda bi: (bi, 0, 0)),
                  pl.BlockSpec((1, t, LANES), lambda bi: (bi, 0, 1)),
                  full(w1p), full(w2p), full(pe), full(w1cat)],
        out_specs=pl.BlockSpec((1, n, 2 * LANES), lambda bi: (bi, 0, 0)),
        compiler_params=_cparams(("arbitrary",)), name="cmp_prompt",
    )(nsa_rows, nsa_rows, w1p, w2p, pe, w1cat)


def _t5_bucket(dist):
    dist = jnp.maximum(dist, 0)
    exact = N_BUCKETS // 2
    ratio = jnp.log(jnp.maximum(dist, 1).astype(F32) / exact) / math.log(MAX_DISTANCE / exact)
    large = jnp.minimum(exact + (ratio * (N_BUCKETS - exact)).astype(jnp.int32), N_BUCKETS - 1)
    return jnp.where(dist < exact, dist, large)


def _bias_at(rel_bias, dist):
    onehot = (_t5_bucket(dist)[..., None] == jnp.arange(N_BUCKETS)).astype(F32)
    return jnp.einsum("...k,kh->h...", onehot, rel_bias.astype(F32), precision=lax.Precision.HIGHEST)


def _overlap(n_c, n_slc, n_slc_pad):
    c = np.arange(n_c)
    j = np.arange(n_slc_pad)
    c_start, c_end = c * CMP_STRIDE, c * CMP_STRIDE + CMP_LEN - 1
    s_start = j * SLC_BLOCK
    ov = ((c_end[:, None] >= s_start[None, :]) & (c_start[:, None] <= s_start[None, :] + SLC_BLOCK - 1)
          & (c[:, None] < n_c - 1) & (j[None, :] < n_slc))
    return jnp.asarray(ov.astype(np.float32), dtype=BF16)


def _expand(n_blk_pad, n_keys, first_block=0):
    blk = np.arange(n_blk_pad)
    s = np.arange(n_keys)
    return jnp.asarray((s[None, :] // SLC_BLOCK == blk[:, None] + first_block).astype(np.float32), dtype=BF16)


def _gate_expand():
    e = np.zeros((NSA_KV, LANES, 3 * NSA_REP * HEAD_DIM), np.float32)
    for g in range(NSA_KV):
        for r in range(NSA_REP):
            for j in range(3):
                e[g, (g * NSA_REP + r) * 3 + j, j * NSA_REP * HEAD_DIM + r * HEAD_DIM:][:HEAD_DIM] = 1.0
    return jnp.asarray(e, dtype=BF16)


def _top_blocks(imp_t, cur_t, n_slc, n_top):
    nb, tq = imp_t.shape
    blk = lax.broadcasted_iota(jnp.int32, (nb, tq), 0)
    valid = blk <= cur_t
    forced = valid & ((blk == 0) | (blk >= cur_t - 1))
    score = jnp.where(forced, FORCE, jnp.where(valid, imp_t, -FORCE))
    rank = jnp.zeros((nb, tq), F32)
    for i in range(n_slc):
        si = score[i:i + 1, :]
        beats = (si > score) | ((si == score) & (blk > i))
        rank = rank + jnp.where(beats, 1.0, 0.0)
    sel_t = jnp.where((rank < n_top) & valid, 1.0, 0.0)
    return jnp.concatenate([sel_t, jnp.zeros((LANES - nb, tq), F32)], axis=0).T


def _nsa_prompt_kernel(q_ref, kc_ref, vc_ref, ks_ref, vs_ref, kw_ref, vw_ref, gate_ref, ge_ref, bc_ref,
                       strip_ref, ovl_ref, ex_ref, o_ref, *, tq, tk, n_c, n_slc, n_top, m_off):
    qi = pl.program_id(2)
    q0 = qi * tq
    r4 = NSA_REP * tq
    q = (q_ref[0, 0] * SCALE).reshape(r4, HEAD_DIM).astype(BF16)
    t_pos = q0 + lax.broadcasted_iota(jnp.int32, (tq, 1), 0)

    c_idx = lax.broadcasted_iota(jnp.int32, (1, n_c), 1)
    mc = ((CMP_STRIDE * c_idx + CMP_LEN - 1) <= t_pos) & (c_idx < n_c - 1)
    lc = _dot_nt(q, kc_ref[0, 0]).reshape(NSA_REP, tq, n_c) + bc_ref[0]
    pc = _masked_softmax(lc, mc[None])
    oc = _dot(pc.reshape(r4, n_c), vc_ref[0, 0])

    pcs_hi, pcs_lo = _split(jnp.sum(pc, axis=0))
    imp_t = _dot_nt(ovl_ref[...], pcs_hi) + _dot_nt(ovl_ref[...], pcs_lo)
    cur_t = (q0 + lax.broadcasted_iota(jnp.int32, (1, tq), 1)) >> 6
    sel = _top_blocks(imp_t, cur_t, n_slc, n_top).astype(BF16)

    def bias_tile(s0, span):
        parts = [strip_ref[0, :, :, pl.ds(pl.multiple_of(s0 - q0 - LANES * h + m_off, LANES), span)]
                 for h in range(tq // LANES)]
        return parts[0] if len(parts) == 1 else jnp.concatenate(parts, axis=1)

    def slc_tile(c, carry):
        m, l, acc = carry
        s0 = pl.multiple_of(c * tk, tk)
        ls = _dot_nt(q, ks_ref[0, 0, pl.ds(s0, tk), :]).reshape(NSA_REP, tq, tk)
        ls = ls + bias_tile(s0, tk)
        selk = jnp.dot(sel, ex_ref[:, pl.ds(s0, tk)], preferred_element_type=F32)
        s_pos = s0 + lax.broadcasted_iota(jnp.int32, (1, tk), 1)
        mk = ((selk > 0.5) & (s_pos <= t_pos))[None]
        lg = jnp.where(mk, ls, NEG_INF)
        m_new = jnp.maximum(m, jnp.max(lg, axis=-1, keepdims=True))
        alpha = jnp.exp(m - m_new)
        p = jnp.where(mk, jnp.exp(lg - m_new), 0.0)
        l = alpha * l + jnp.sum(p, axis=-1, keepdims=True)
        pv = _dot(p.reshape(r4, tk), vs_ref[0, 0, pl.ds(s0, tk), :]).reshape(NSA_REP, tq, HEAD_DIM)
        return m_new, l, alpha * acc + pv

    n_tiles = (q0 + tq + tk - 1) // tk
    m, l, acc = lax.fori_loop(0, n_tiles, slc_tile,
                              (jnp.full((NSA_REP, tq, 1), NEG_INF, F32), jnp.zeros((NSA_REP, tq, 1), F32),
                               jnp.zeros((NSA_REP, tq, HEAD_DIM), F32)))
    o_s = (acc / jnp.maximum(l, TINY)).reshape(r4, HEAD_DIM)

    band = WINDOW + tq
    w0 = pl.multiple_of(jnp.maximum(q0 - WINDOW, 0), LANES)
    lw = _dot_nt(q, kw_ref[0, 0, pl.ds(w0, band), :]).reshape(NSA_REP, tq, band)
    lw = lw + bias_tile(w0, band)
    dist = t_pos - (w0 + lax.broadcasted_iota(jnp.int32, (1, band), 1))
    pw = _masked_softmax(lw, ((dist >= 0) & (dist < WINDOW))[None])
    o_w = _dot(pw.reshape(r4, band), vw_ref[0, 0, pl.ds(w0, band), :])

    gexp = _dot_hl(_sigmoid(gate_ref[0]), ge_ref[0])
    gw = NSA_REP * HEAD_DIM

    def token_major(o):
        return jnp.concatenate([o[r * tq:(r + 1) * tq] for r in range(NSA_REP)], axis=1)

    o_ref[0] = (gexp[:, 0:gw] * token_major(oc) + gexp[:, gw:2 * gw] * token_major(o_s)
                + gexp[:, 2 * gw:3 * gw] * token_major(o_w))


def _nsa_prompt(q_nsa, nsa_rows, win_rows, gates, kcvc, rel_bias, tq, tk):
    b, t, _ = q_nsa.shape
    n_c = t // CMP_STRIDE
    n_slc = -(-t // SLC_BLOCK)
    n_top = min(SLC_TOP, n_slc)
    assert n_slc <= LANES and t % tk == 0 and t >= WINDOW + tq and tq % LANES == 0
    g_, r_ = NSA_KV, NSA_REP
    qh = q_nsa.reshape(b, t, g_, r_, HEAD_DIM).transpose(0, 2, 3, 1, 4)
    nsah = nsa_rows.reshape(b, t, 4, g_, HEAD_DIM).transpose(2, 0, 3, 1, 4)
    winh = win_rows.reshape(b, t, 2, g_, HEAD_DIM).transpose(2, 0, 3, 1, 4)
    kch = kcvc.reshape(b, n_c, 2, g_, HEAD_DIM).transpose(2, 0, 3, 1, 4)
    m_off = t - LANES
    width = m_off + max(tk, WINDOW + tq)
    tt = jnp.arange(t)
    bias_c = _bias_at(rel_bias, tt[:, None] - (jnp.arange(n_c)[None, :] * CMP_STRIDE + CMP_LEN - 1))
    bias_c = bias_c.reshape(g_, r_, t, n_c)
    strip = _bias_at(rel_bias, jnp.arange(LANES)[:, None] + m_off - jnp.arange(width)[None, :])
    strip = strip.reshape(g_, r_, LANES, width)
    kv_spec = pl.BlockSpec((1, 1, t, HEAD_DIM), lambda bi, g, qi: (bi, g, 0, 0))
    kc_spec = pl.BlockSpec((1, 1, n_c, HEAD_DIM), lambda bi, g, qi: (bi, g, 0, 0))
    nb = -(-n_slc // 8) * 8
    ovl = _overlap(n_c, n_slc, nb).T
    ex = _expand(LANES, t)
    ge = _gate_expand()
    kern = functools.partial(_nsa_prompt_kernel, tq=tq, tk=tk, n_c=n_c, n_slc=n_slc, n_top=n_top, m_off=m_off)
    return pl.pallas_call(
        kern, out_shape=SDS((b, t, NSA_W), F32), grid=(b, g_, t // tq),
        in_specs=[pl.BlockSpec((1, 1, r_, tq, HEAD_DIM), lambda bi, g, qi: (bi, g, 0, qi, 0)),
                  kc_spec, kc_spec, kv_spec, kv_spec, kv_spec, kv_spec,
                  pl.BlockSpec((1, tq, LANES), lambda bi, g, qi: (bi, qi, 0)),
                  pl.BlockSpec((1,) + ge.shape[1:], lambda bi, g, qi: (g, 0, 0)),
                  pl.BlockSpec((1, r_, tq, n_c), lambda bi, g, qi: (g, 0, qi, 0)),
                  pl.BlockSpec((1, r_, LANES, width), lambda bi, g, qi: (g, 0, 0, 0)),
                  pl.BlockSpec(ovl.shape, lambda bi, g, qi: (0, 0)),
                  pl.BlockSpec(ex.shape, lambda bi, g, qi: (0, 0))],
        out_specs=pl.BlockSpec((1, tq, r_ * HEAD_DIM), lambda bi, g, qi: (bi, qi, g)),
        compiler_params=_cparams(("arbitrary", "arbitrary", "arbitrary")), name="nsa_prompt",
    )(qh, kch[0], kch[1], nsah[2], nsah[3], winh[0], winh[1], gates, ge, bias_c, strip, ovl, ex)


def _tokens_last(cache):
    return cache.transpose(0, 1, 3, 4, 5, 2)


def _page_copies(page_src, pt_ref, seq, first_page, n_pages, buf, slot, sem):
    return [pltpu.make_async_copy(page_src(pt_ref[seq, first_page + i]), buf.at[slot, i], sem.at[slot])
            for i in range(n_pages)]


def _paged_pipeline(page_src, pt_ref, n_seq, n_chunk, chunk_pages, buf, sem, reverse):
    seq, j = pl.program_id(0), pl.program_id(1)
    step = seq * n_chunk + j

    def start(st):
        s_, j_ = st // n_chunk, st % n_chunk
        first = ((n_chunk - 1 - j_) if reverse else j_) * chunk_pages
        for cp in _page_copies(page_src, pt_ref, s_, first, chunk_pages, buf, st % 2, sem):
            cp.start()

    @pl.when(step == 0)
    def _():
        start(step)

    @pl.when(step + 1 < n_seq * n_chunk)
    def _():
        start(step + 1)

    slot = step % 2
    for cp in _page_copies(page_src, pt_ref, 0, 0, chunk_pages, buf, slot, sem):
        cp.wait()
    return slot


def _sb_sample_kernel(pt_ref, qbd_ref, new_ref, tri_ref, cache_ref, o_ref, buf, sem, carry_scr, acc_scr,
                      *, layer, n_seq, n_chunk, chunk_pages, page_rows, ts):
    j = pl.program_id(1)
    slot = _paged_pipeline(lambda page: cache_ref.at[layer, page], pt_ref, n_seq, n_chunk, chunk_pages, buf,
                           sem, reverse=True)
    rows = ts * SB_HEADS
    qbd = (qbd_ref[0] * SCALE).astype(BF16)
    tri = tri_ref[...]
    sub = tri.shape[0]

    def cumsum_tiles(lb, lk, carry, mask):
        atts = []
        for s in reversed(range(lb.shape[1] // sub)):
            sl = slice(s * sub, (s + 1) * sub)
            att = jnp.exp(lb[:, sl] + _dot_hl(lk[:, sl], tri) + carry)
            atts.append(att if mask is None else jnp.where(mask[:, sl], att, 0.0))
            carry = carry + jnp.sum(lk[:, sl], axis=1, keepdims=True)
        return jnp.concatenate(atts[::-1], axis=1), carry

    @pl.when(j == 0)
    def _():
        n_new = new_ref.shape[1]
        lb, lk = _stick_terms(_dot_nt(qbd, new_ref[0, :, 0:SB_W]))
        q_idx = lax.broadcasted_iota(jnp.int32, (rows, n_new), 0) // SB_HEADS
        causal = lax.broadcasted_iota(jnp.int32, (rows, n_new), 1) < q_idx
        lk = jnp.where(causal, lk, 0.0)
        att, carry = cumsum_tiles(lb, lk, jnp.zeros((rows, 1), F32), causal)
        carry_scr[...] = carry
        acc_scr[...] = _dot(att, new_ref[0, :, SB_W:2 * SB_W])

    z = jnp.concatenate([_dot(qbd, buf[slot, i, 0].reshape(SB_W, page_rows)) for i in range(chunk_pages)],
                        axis=1)
    lb, lk = _stick_terms(z)
    att, carry = cumsum_tiles(lb, lk, carry_scr[...], None)
    carry_scr[...] = carry
    acc = acc_scr[...]
    for i in range(chunk_pages):
        acc = acc + _dot_nt(att[:, i * page_rows:(i + 1) * page_rows], buf[slot, i, 1].reshape(SB_W, page_rows))
    acc_scr[...] = acc

    @pl.when(j == n_chunk - 1)
    def _():
        acc = acc_scr[...]
        head_of_row = lax.broadcasted_iota(jnp.int32, acc.shape, 0) % SB_HEADS
        head_of_lane = lax.broadcasted_iota(jnp.int32, acc.shape, 1) // HEAD_DIM
        own = jnp.where(head_of_row == head_of_lane, acc, 0.0)
        o_ref[0] = jnp.sum(own.reshape(ts, SB_HEADS, SB_W), axis=1)


def _sb_sample(layer, q_sb, sb_rows, cache_sb, page_table, chunk_pages, sub):
    bs, ts, _ = q_sb.shape
    n_pages = page_table.shape[1]
    page_rows = cache_sb.shape[-1]
    n_chunk = n_pages // chunk_pages
    rows = ts * SB_HEADS
    eye = jnp.eye(SB_HEADS, dtype=F32)
    qbd = jnp.einsum("bqhd,hk->bqhkd", q_sb.reshape(bs, ts, SB_HEADS, HEAD_DIM), eye).reshape(bs, rows, SB_W)
    new = jnp.pad(sb_rows, ((0, 0), (0, sub - ts), (0, 0)))
    kern = functools.partial(_sb_sample_kernel, layer=layer, n_seq=bs, n_chunk=n_chunk,
                             chunk_pages=chunk_pages, page_rows=page_rows, ts=ts)
    grid_spec = pltpu.PrefetchScalarGridSpec(
        num_scalar_prefetch=1, grid=(bs, n_chunk),
        in_specs=[pl.BlockSpec((1, rows, SB_W), lambda b, j, pt: (b, 0, 0)),
                  pl.BlockSpec((1, sub, 2 * SB_W), lambda b, j, pt: (b, 0, 0)),
                  pl.BlockSpec((sub, sub), lambda b, j, pt: (0, 0)),
                  pl.BlockSpec(memory_space=pl.ANY)],
        out_specs=pl.BlockSpec((1, ts, SB_W), lambda b, j, pt: (b, 0, 0)),
        scratch_shapes=[pltpu.VMEM((2, chunk_pages) + cache_sb.shape[2:], F32),
                        pltpu.SemaphoreType.DMA((2,)),
                        pltpu.VMEM((rows, 1), F32), pltpu.VMEM((rows, SB_W), F32)])
    return pl.pallas_call(
        kern, out_shape=SDS((bs, ts, SB_W), F32), grid_spec=grid_spec,
        compiler_params=_cparams(("arbitrary", "arbitrary")), name="sb_sample",
    )(page_table, qbd, new, _tri(sub), cache_sb)


def _cmp_sample_kernel(pt_ref, w1p_ref, w2p_ref, pe_ref, w1cat_ref, cache_ref, o_ref, buf, sem, slab_scr, u_scr,
                       *, layer, n_seq, n_chunk, chunk_pages, page_rows):
    j = pl.program_id(1)
    slot = _paged_pipeline(lambda page: cache_ref.at[layer, page, pl.ds(0, 2)], pt_ref, n_seq, n_chunk,
                           chunk_pages, buf, sem, reverse=False)

    def to_rows(i, _):
        for kind in range(2):
            slab_scr[kind, pl.ds(pl.multiple_of(i * page_rows, page_rows), page_rows), :] = (
                buf[slot, i, kind].reshape(LANES, page_rows).T)
        return 0

    lax.fori_loop(0, chunk_pages, to_rows, 0)
    n_step = chunk_pages * page_rows // CMP_STRIDE
    for kind in range(2):
        u_scr[kind, pl.ds(pl.multiple_of(j * n_step, n_step), n_step), :] = _compress_partial(
            slab_scr.at[kind], w1p_ref, kind, n_step)

    @pl.when(j == n_chunk - 1)
    def _():
        for kind in range(2):
            o_ref[0, :, kind * LANES:(kind + 1) * LANES] = _compress_finish(
                u_scr[kind], kind, w2p_ref, pe_ref, w1cat_ref)


def _cmp_sample(layer, cache_nsa, page_table, cw, chunk_pages):
    bs, n_pages = page_table.shape
    page_rows = cache_nsa.shape[-1]
    assert page_rows == LANES and KV_W == LANES
    n_chunk = n_pages // chunk_pages
    n_c = n_pages * page_rows // CMP_STRIDE
    w1p, w2p, pe, w1cat = cw
    full = lambda a: pl.BlockSpec(a.shape, lambda b, j, pt: (0,) * a.ndim)
    kern = functools.partial(_cmp_sample_kernel, layer=layer, n_seq=bs, n_chunk=n_chunk,
                             chunk_pages=chunk_pages, page_rows=page_rows)
    grid_spec = pltpu.PrefetchScalarGridSpec(
        num_scalar_prefetch=1, grid=(bs, n_chunk),
        in_specs=[full(w1p), full(w2p), full(pe), full(w1cat), pl.BlockSpec(memory_space=pl.ANY)],
        out_specs=pl.BlockSpec((1, n_c, 2 * LANES), lambda b, j, pt: (b, 0, 0)),
        scratch_shapes=[pltpu.VMEM((2, chunk_pages, 2) + cache_nsa.shape[3:], F32),
                        pltpu.SemaphoreType.DMA((2,)),
                        pltpu.VMEM((2, chunk_pages * page_rows, LANES), F32),
                        pltpu.VMEM((2, n_c, 2 * LANES), F32)])
    return pl.pallas_call(
        kern, out_shape=SDS((bs, n_c, 2 * LANES), F32), grid_spec=grid_spec,
        compiler_params=_cparams(("arbitrary", "arbitrary")), name="cmp_sample",
    )(page_table, w1p, w2p, pe, w1cat, cache_nsa)


def _rank_rows(score, n_top):
    rows, nb = score.shape
    ii = lax.broadcasted_iota(jnp.int32, (nb, nb), 0)
    jj = lax.broadcasted_iota(jnp.int32, (nb, nb), 1)
    out = []
    for r in range(rows):
        s_row = score[r:r + 1, :]
        s_col = jnp.sum(jnp.where(ii == jj, s_row, 0.0), axis=1, keepdims=True)
        beats = (s_col > s_row) | ((s_col == s_row) & (ii < jj))
        rank = jnp.sum(jnp.where(beats, 1.0, 0.0), axis=0, keepdims=True)
        out.append(jnp.where(rank < n_top, 1.0, 0.0))
    return jnp.concatenate(out, axis=0)


def _nsa_a_kernel(qbd_ref, kcvc_ref, win_ref, wnew_ref, bc_ref, bwp_ref, bwn_ref, ovl_ref,
                  oc_ref, ow_ref, sel_ref, *, ts, past, n_slc, n_top):
    rows = NSA_REP * NSA_KV * ts
    gq = NSA_KV * ts
    qbd = (qbd_ref[0] * SCALE).astype(BF16)
    q_idx = lax.broadcasted_iota(jnp.int32, (rows, 1), 0) % ts
    t_pos = past + q_idx

    n_c = kcvc_ref.shape[1]
    c_idx = lax.broadcasted_iota(jnp.int32, (1, n_c), 1)
    mc = ((CMP_STRIDE * c_idx + CMP_LEN - 1) <= t_pos) & (c_idx < n_c - 1)
    pc = _masked_softmax(_dot_nt(qbd, kcvc_ref[0, :, 0:KV_W]) + bc_ref[...], mc)
    oc_ref[0] = _dot(pc, kcvc_ref[0, :, KV_W:2 * KV_W])

    pcs = pc[0:gq]
    for r in range(1, NSA_REP):
        pcs = pcs + pc[r * gq:(r + 1) * gq]
    imp = _dot_hl(pcs, ovl_ref[...])
    nb = imp.shape[1]
    blk = lax.broadcasted_iota(jnp.int32, (gq, nb), 1)
    cur = (past + lax.broadcasted_iota(jnp.int32, (gq, 1), 0) % ts) >> 6
    valid = blk <= cur
    forced = valid & ((blk == 0) | (blk >= cur - 1))
    score = jnp.where(forced, FORCE, jnp.where(valid, imp, -FORCE))
    sel_ref[0] = jnp.where(valid, _rank_rows(score, n_top), 0.0)

    w_buf = win_ref.shape[-1]
    r_idx = lax.broadcasted_iota(jnp.int32, (1, w_buf), 1)
    d_past = w_buf + q_idx - r_idx
    l1 = _dot(qbd, win_ref[0, 0, 0].reshape(KV_W, w_buf)) + bwp_ref[...]
    m1 = (d_past >= 0) & (d_past < WINDOW)
    n_new = wnew_ref.shape[1]
    d_new = q_idx - lax.broadcasted_iota(jnp.int32, (1, n_new), 1)
    l2 = _dot_nt(qbd, wnew_ref[0, :, 0:KV_W]) + bwn_ref[...]
    m2 = (d_new >= 0) & (d_new < WINDOW)
    l1 = jnp.where(m1, l1, NEG_INF)
    l2 = jnp.where(m2, l2, NEG_INF)
    m = jnp.maximum(jnp.max(l1, axis=1, keepdims=True), jnp.max(l2, axis=1, keepdims=True))
    e1 = jnp.where(m1, jnp.exp(l1 - m), 0.0)
    e2 = jnp.where(m2, jnp.exp(l2 - m), 0.0)
    den = jnp.maximum(jnp.sum(e1, axis=1, keepdims=True) + jnp.sum(e2, axis=1, keepdims=True), TINY)
    ow_ref[0] = (_dot_nt(e1 / den, win_ref[0, 0, 1].reshape(KV_W, w_buf))
                 + _dot(e2 / den, wnew_ref[0, :, KV_W:2 * KV_W]))


def _nsa_b_kernel(pt_ref, qbd_ref, snew_ref, sel_ref, selnew_ref, bs_ref, bsn_ref, ex_ref, oc_ref, ow_ref,
                  gate_ref, cache_ref, o_ref, buf, sem, m_scr, l_scr, acc_scr,
                  *, layer, n_seq, n_chunk, chunk_pages, page_rows, ts):
    j = pl.program_id(1)
    slot = _paged_pipeline(lambda page: cache_ref.at[layer, page, pl.ds(2, 2)], pt_ref, n_seq, n_chunk,
                           chunk_pages, buf, sem, reverse=False)
    rows = NSA_REP * NSA_KV * ts
    qbd = (qbd_ref[0] * SCALE).astype(BF16)
    q_idx = lax.broadcasted_iota(jnp.int32, (rows, 1), 0) % ts

    def online(lg, mk, weigh):
        lg = jnp.where(mk, lg, NEG_INF)
        m_old = m_scr[...]
        m_new = jnp.maximum(m_old, jnp.max(lg, axis=1, keepdims=True))
        alpha = jnp.exp(m_old - m_new)
        p = jnp.where(mk, jnp.exp(lg - m_new), 0.0)
        m_scr[...] = m_new
        l_scr[...] = alpha * l_scr[...] + jnp.sum(p, axis=1, keepdims=True)
        acc_scr[...] = alpha * acc_scr[...] + weigh(p)

    def rows_of(x):
        return jnp.concatenate([x] * NSA_REP, axis=0)

    @pl.when(j == 0)
    def _():
        m_scr[...] = jnp.full_like(m_scr, NEG_INF)
        l_scr[...] = jnp.zeros_like(l_scr)
        acc_scr[...] = jnp.zeros_like(acc_scr)
        n_new = snew_ref.shape[1]
        d_new = q_idx - lax.broadcasted_iota(jnp.int32, (1, n_new), 1)
        chosen = rows_of(selnew_ref[0])[:, 0:1] > 0.5
        online(_dot_nt(qbd, snew_ref[0, :, 0:KV_W]) + bsn_ref[...], chosen & (d_new >= 0),
               lambda p: _dot(p, snew_ref[0, :, KV_W:2 * KV_W]))

    selk = rows_of(jnp.dot(sel_ref[0, 0].astype(BF16), ex_ref[...], preferred_element_type=F32))
    def all_pages(kind):
        return jnp.concatenate([buf[slot, i, kind].reshape(KV_W, page_rows).astype(BF16)
                                for i in range(chunk_pages)], axis=1)

    lg = jnp.dot(qbd, all_pages(0), preferred_element_type=F32)
    online(lg + bs_ref[0], selk > 0.5, lambda p: _dot_nt(p, all_pages(1)))

    @pl.when(j == n_chunk - 1)
    def _():
        o_s = acc_scr[...] / jnp.maximum(l_scr[...], TINY)
        g = _sigmoid(gate_ref[0])
        o_ref[0] = g[:, 0:1] * oc_ref[0] + g[:, 1:2] * o_s + g[:, 2:3] * ow_ref[0]


def _nsa_sample(layer, q_nsa, nsa_rows, win_rows, gates, kcvc, state_win, cache_nsa, page_table, rel_bias,
                chunk_pages):
    bs, ts, _ = q_nsa.shape
    n_pages = page_table.shape[1]
    page_rows = cache_nsa.shape[-1]
    past = n_pages * page_rows
    n_c = kcvc.shape[1]
    w_buf = state_win.shape[-1]
    n_slc = -(-(past + ts) // SLC_BLOCK)
    n_top = min(SLC_TOP, n_slc)
    n_past_blk = past // SLC_BLOCK
    nb_pad = -(-n_slc // LANES) * LANES
    g_, r_ = NSA_KV, NSA_REP
    rows = r_ * g_ * ts
    new_pad = LANES
    assert past % SLC_BLOCK == 0 and ts <= SLC_BLOCK and w_buf == WINDOW

    q5 = q_nsa.reshape(bs, ts, g_, r_, HEAD_DIM)
    qbd = jnp.einsum("bqgrd,gk->brgqkd", q5, jnp.eye(g_, dtype=F32)).reshape(bs, rows, KV_W)
    qq = jnp.arange(ts)

    def row_bias(dist_qk):
        bias = _bias_at(rel_bias, dist_qk).reshape(g_, r_, ts, -1)
        return bias.transpose(1, 0, 2, 3).reshape(rows, -1)

    bias_c = row_bias(past + qq[:, None] - (jnp.arange(n_c)[None, :] * CMP_STRIDE + CMP_LEN - 1))
    bias_wp = row_bias(w_buf + qq[:, None] - jnp.arange(w_buf)[None, :])
    bias_new = row_bias(qq[:, None] - jnp.arange(new_pad)[None, :])
    chunk_keys = chunk_pages * page_rows
    n_chunk = n_pages // chunk_pages
    bias_s = row_bias(past + qq[:, None] - jnp.arange(past)[None, :])
    bias_s = bias_s.reshape(rows, n_chunk, chunk_keys).transpose(1, 0, 2)
    ovl = _overlap(n_c, n_slc, nb_pad)
    wnew = jnp.pad(win_rows, ((0, 0), (0, new_pad - ts), (0, 0)))
    snew = jnp.pad(nsa_rows[:, :, 2 * KV_W:4 * KV_W], ((0, 0), (0, new_pad - ts), (0, 0)))

    full = lambda a: pl.BlockSpec(a.shape, lambda b: (0,) * a.ndim)
    oc, ow, sel = pl.pallas_call(
        functools.partial(_nsa_a_kernel, ts=ts, past=past, n_slc=n_slc, n_top=n_top),
        out_shape=[SDS((bs, rows, KV_W), F32), SDS((bs, rows, KV_W), F32), SDS((bs, g_ * ts, nb_pad), F32)],
        grid=(bs,),
        in_specs=[pl.BlockSpec((1, rows, KV_W), lambda b: (b, 0, 0)),
                  pl.BlockSpec((1, n_c, 2 * KV_W), lambda b: (b, 0, 0)),
                  pl.BlockSpec((1, 1) + state_win.shape[2:], lambda b: (layer, b, 0, 0, 0, 0)),
                  pl.BlockSpec((1, new_pad, 2 * KV_W), lambda b: (b, 0, 0)),
                  full(bias_c), full(bias_wp), full(bias_new), full(ovl)],
        out_specs=[pl.BlockSpec((1, rows, KV_W), lambda b: (b, 0, 0)),
                   pl.BlockSpec((1, rows, KV_W), lambda b: (b, 0, 0)),
                   pl.BlockSpec((1, g_ * ts, nb_pad), lambda b: (b, 0, 0))],
        compiler_params=_cparams(("arbitrary",)), name="nsa_sample_a",
    )(qbd, kcvc, state_win, wnew, bias_c, bias_wp, bias_new, ovl)

    blk_per_chunk = chunk_keys // SLC_BLOCK
    sel_chunks = sel[:, :, :n_past_blk].reshape(bs, g_ * ts, n_chunk, blk_per_chunk).transpose(0, 2, 1, 3)
    sel_chunks = jnp.pad(sel_chunks, ((0, 0), (0, 0), (0, 0), (0, LANES - blk_per_chunk)))
    sel_new = jnp.pad(sel[:, :, n_past_blk:n_past_blk + 1], ((0, 0), (0, 0), (0, LANES - 1)))
    ex = _expand(LANES, chunk_keys)
    gt = gates[:, :, :GATE_W].reshape(bs, ts, g_, r_, 3).transpose(0, 3, 2, 1, 4).reshape(bs, rows, 3)
    gt = jnp.pad(gt, ((0, 0), (0, 0), (0, LANES - 3)))

    kern = functools.partial(_nsa_b_kernel, layer=layer, n_seq=bs, n_chunk=n_chunk, chunk_pages=chunk_pages,
                             page_rows=page_rows, ts=ts)
    per_seq = lambda a: pl.BlockSpec((1,) + a.shape[1:], lambda b, j, pt: (b,) + (0,) * (a.ndim - 1))
    const = lambda a: pl.BlockSpec(a.shape, lambda b, j, pt: (0,) * a.ndim)
    grid_spec = pltpu.PrefetchScalarGridSpec(
        num_scalar_prefetch=1, grid=(bs, n_chunk),
        in_specs=[per_seq(qbd), per_seq(snew),
                  pl.BlockSpec((1, 1, g_ * ts, LANES), lambda b, j, pt: (b, j, 0, 0)),
                  per_seq(sel_new),
                  pl.BlockSpec((1, rows, chunk_keys), lambda b, j, pt: (j, 0, 0)),
                  const(bias_new), const(ex), per_seq(oc), per_seq(ow), per_seq(gt),
                  pl.BlockSpec(memory_space=pl.ANY)],
        out_specs=pl.BlockSpec((1, rows, KV_W), lambda b, j, pt: (b, 0, 0)),
        scratch_shapes=[pltpu.VMEM((2, chunk_pages, 2) + cache_nsa.shape[3:], F32),
                        pltpu.SemaphoreType.DMA((2,)),
                        pltpu.VMEM((rows, 1), F32), pltpu.VMEM((rows, 1), F32), pltpu.VMEM((rows, KV_W), F32)])
    o = pl.pallas_call(
        kern, out_shape=SDS((bs, rows, KV_W), F32), grid_spec=grid_spec,
        compiler_params=_cparams(("arbitrary", "arbitrary")), name="nsa_sample_b",
    )(page_table, qbd, snew, sel_chunks, sel_new, bias_s, bias_new, ex, oc, ow, gt, cache_nsa)
    o = o.reshape(bs, r_, g_, ts, g_, HEAD_DIM)
    o = jnp.stack([o[:, :, g, :, g, :] for g in range(g_)], axis=2)
    return o.transpose(0, 3, 2, 1, 4).reshape(bs, ts, NSA_W)


def _layer_weights(l, w_in, w_out, w_up, w_down, cmp_pe, cmp_w1, cmp_w2):
    w_in_bf = jnp.pad(w_in[l], ((0, 0), (0, D_IN_PAD - D_IN))).astype(BF16)
    return dict(w_in=w_in_bf, w_out=w_out[l].astype(BF16), w_up=w_up[l].astype(BF16),
                w_down=w_down[l].astype(BF16), cmp=_compress_weights(cmp_pe[l], cmp_w1[l], cmp_w2[l]))


def _dense_tail(o_sb, o_nsa, x2d, mod, lw, g_post_mix, g_pre_ffn, g_post_ffn, tm, tf):
    x1 = _out_proj(o_sb, o_nsa, x2d, g_post_mix, mod, lw["w_out"], tm)
    return _ffn(x1, g_pre_ffn, g_post_ffn, mod, lw["w_up"], lw["w_down"], tm, tf)


def kernel(x_prompt, x_sample, c_prompt, c_sample, cache_sb_kv, cache_nsa_kv, state_nsa_win, page_table, w_ada, b_ada, g_pre_mix, g_post_mix, g_pre_ffn, g_post_ffn, w_in, w_out, cmp_pe, cmp_w1, cmp_w2, rel_bias, w_up, w_down):
    bp, t, d = x_prompt.shape
    bs, ts, _ = x_sample.shape
    depth = w_in.shape[0]
    n_phys, page_rows = cache_sb_kv.shape[1], cache_sb_kv.shape[2]
    n_pages = page_table.shape[1]
    w_buf = state_nsa_win.shape[2]
    f = w_up.shape[2]
    page_table = page_table.astype(jnp.int32)

    cache_sb = _tokens_last(cache_sb_kv)
    cache_nsa = _tokens_last(cache_nsa_kv)
    state_win = _tokens_last(state_nsa_win)

    tm_p = 256
    tm_ffn_p = 512 if t % 512 == 0 else tm_p
    assert t % tm_p == 0
    tf = 1024 if f % 1024 == 0 else f
    tq_sb = 256
    sb_chunk_pages = math.gcd(n_pages, 16)
    cmp_chunk_pages = math.gcd(n_pages, 64)
    slc_chunk_pages = math.gcd(n_pages, 16)

    mod = _ada(jnp.concatenate([c_prompt, c_sample], axis=0), w_ada, b_ada)

    y_p = x_prompt.reshape(bp * t, d)
    y_s = x_sample.reshape(bs * ts, d)
    outs = {k: [] for k in ("sb_p", "sb_s", "nsa_p", "nsa_s", "win_p", "win_s")}
    for l in range(depth):
        lw = _layer_weights(l, w_in, w_out, w_up, w_down, cmp_pe, cmp_w1, cmp_w2)

        mod_p = _Mod(mod[l, :bp].reshape(bp, 1, 6 * d), t)
        q_sb, sb_rows, q_nsa, nsa_rows, win_rows, gates = _in_proj(y_p, g_pre_mix[l], mod_p, lw["w_in"], tm_p)
        o_sb = _sb_prompt(q_sb.reshape(bp, t, SB_W), sb_rows.reshape(bp, t, 2 * SB_W), tq_sb)
        nsa3 = nsa_rows.reshape(bp, t, 4 * KV_W)
        win3 = win_rows.reshape(bp, t, 2 * KV_W)
        kcvc = _cmp_prompt(nsa3, lw["cmp"])
        o_nsa = _nsa_prompt(q_nsa.reshape(bp, t, NSA_W), nsa3, win3, gates.reshape(bp, t, LANES), kcvc,
                            rel_bias, 256, 512 if t % 512 == 0 else 256)
        y_p = _dense_tail(o_sb.reshape(bp * t, SB_W), o_nsa.reshape(bp * t, NSA_W), y_p, mod_p, lw,
                          g_post_mix[l], g_pre_ffn[l], g_post_ffn[l], tm_ffn_p, tf)
        outs["sb_p"].append(sb_rows.reshape(bp, t, 2, SB_HEADS, HEAD_DIM))
        outs["nsa_p"].append(nsa_rows.reshape(bp, t, 4, NSA_KV, HEAD_DIM))
        wp = win3 if t >= w_buf else jnp.pad(win3, ((0, 0), (w_buf - t, 0), (0, 0)))
        outs["win_p"].append(wp[:, wp.shape[1] - w_buf:].reshape(bp, w_buf, 2, NSA_KV, HEAD_DIM))

        tm_s = bs * ts
        mod_s = _Mod(jnp.repeat(mod[l, bp:], ts, axis=0).reshape(1, bs * ts, 6 * d), 1)
        q_sb, sb_rows, q_nsa, nsa_rows, win_rows, gates = _in_proj(y_s, g_pre_mix[l], mod_s, lw["w_in"], tm_s)
        sb3 = sb_rows.reshape(bs, ts, 2 * SB_W)
        nsa3 = nsa_rows.reshape(bs, ts, 4 * KV_W)
        win3 = win_rows.reshape(bs, ts, 2 * KV_W)
        o_sb = _sb_sample(l, q_sb.reshape(bs, ts, SB_W), sb3, cache_sb, page_table, sb_chunk_pages, 256)
        kcvc = _cmp_sample(l, cache_nsa, page_table, lw["cmp"], cmp_chunk_pages)
        o_nsa = _nsa_sample(l, q_nsa.reshape(bs, ts, NSA_W), nsa3, win3, gates.reshape(bs, ts, LANES), kcvc,
                            state_win, cache_nsa, page_table, rel_bias, slc_chunk_pages)
        y_s = _dense_tail(o_sb.reshape(bs * ts, SB_W), o_nsa.reshape(bs * ts, NSA_W), y_s, mod_s, lw,
                          g_post_mix[l], g_pre_ffn[l], g_post_ffn[l], tm_s, tf)
        outs["sb_s"].append(sb_rows.reshape(bs, ts, 2, SB_HEADS, HEAD_DIM))
        outs["nsa_s"].append(nsa_rows.reshape(bs, ts, 4, NSA_KV, HEAD_DIM))
        win_all = jnp.concatenate([state_nsa_win[l], win3.reshape(bs, ts, 2, NSA_KV, HEAD_DIM)], axis=1)
        outs["win_s"].append(win_all[:, win_all.shape[1] - w_buf:])

    return (y_p.reshape(bp, t, d), y_s.reshape(bs, ts, d), jnp.stack(outs["sb_p"]), jnp.stack(outs["sb_s"]),
            jnp.stack(outs["nsa_p"]), jnp.stack(outs["nsa_s"]), jnp.stack(outs["win_p"]),
            jnp.stack(outs["win_s"]))
```
